```python
import jax, jax.numpy as jnp
from jax import lax
import numpy as np

D_MODEL = 1024
BATCH = 4
SEQ = 8192
DEPTH = 4

MIXER_ORDER = ('dsa', 'moba', 'retention')
N_MIXERS = 3
ROPE_THETA = 10000.0
ALPHA = (2 * DEPTH) ** 0.25
BETA = (8 * DEPTH) ** -0.25
LN_EPS = 1e-5
GN_EPS = 1e-6
NEG_INF = -1e30

DSA_HEADS = 16
DSA_HEAD_DIM = D_MODEL // DSA_HEADS
IDX_HEADS = 8
IDX_DIM = 64
DSA_TOPK = 256
DSA_Q_BLOCK = 128

MOBA_HEADS = 16
MOBA_HEAD_DIM = D_MODEL // MOBA_HEADS
MOBA_BLOCK = 256
MOBA_TOPK = 3
MOBA_Q_CHUNK = 32

RET_HEADS = 4
RET_QK_DIM = D_MODEL // RET_HEADS
RET_V_DIM = 2 * RET_QK_DIM
RET_CHUNK = 128

D_FF = 2816
CONV_WIDTH = 3

kernel_name = 'hybrid_dsa_moba_retention_deepnorm'


def _rope_cos_sin(seq_len, dim):
    inv = 1.0 / (ROPE_THETA ** (jnp.arange(0, dim, 2, dtype=jnp.float32) / dim))
    ang = jnp.arange(seq_len, dtype=jnp.float32)[:, None] * inv[None, :]
    return jnp.cos(ang), jnp.sin(ang)


def _apply_rope(x, cos, sin):
    half = x.shape[-1] // 2
    xf = x.astype(jnp.float32)
    x1, x2 = xf[..., :half], xf[..., half:]
    c, s = cos[:, None, :], sin[:, None, :]
    return jnp.concatenate([x1 * c - x2 * s, x1 * s + x2 * c], axis=-1).astype(x.dtype)


def _layer_norm(x, g, b):
    xf = x.astype(jnp.float32)
    mu = jnp.mean(xf, axis=-1, keepdims=True)
    var = jnp.mean(jnp.square(xf - mu), axis=-1, keepdims=True)
    return ((xf - mu) * lax.rsqrt(var + LN_EPS) * g + b).astype(x.dtype)


def _dsa_mixer(x, w_in, w_out):
    bsz, seq, _ = x.shape
    h, dh, hi, di = DSA_HEADS, DSA_HEAD_DIM, IDX_HEADS, IDX_DIM
    sizes = [h * dh, dh, dh, hi * di, di, hi]
    offs = np.cumsum(sizes)[:-1].tolist()
    q, k, v, qi, ki, wi = jnp.split(x @ w_in, offs, axis=-1)
    cos, sin = _rope_cos_sin(seq, dh)
    ci, si = _rope_cos_sin(seq, di)
    q = _apply_rope(q.reshape(bsz, seq, h, dh), cos, sin)
    k = _apply_rope(k.reshape(bsz, seq, 1, dh), cos, sin)[:, :, 0]
    qi = _apply_rope(qi.reshape(bsz, seq, hi, di), ci, si)
    ki = _apply_rope(ki.reshape(bsz, seq, 1, di), ci, si)[:, :, 0]
    wi = wi * (hi ** -0.5 * di ** -0.5)
    topk = min(DSA_TOPK, seq // 4)
    nblk = seq // DSA_Q_BLOCK
    key_pos = jnp.arange(seq)
    gather_rows = jax.vmap(lambda tab, idx: tab[idx])
    scale = dh ** -0.5

    def to_blocks(t):
        return t.reshape((bsz, nblk, DSA_Q_BLOCK) + t.shape[2:]).swapaxes(0, 1)

    def block(args):
        qb, qib, wib, qpos = args
        causal = key_pos[None, :] <= qpos[:, None]
        logits = jnp.einsum('bqhd,bsd->bqhs', qib, ki).astype(jnp.float32)
        index = jnp.einsum('bqh,bqhs->bqs', wib.astype(jnp.float32), jax.nn.relu(logits))
        index = jnp.where(causal[None], index, NEG_INF)
        _, sel = lax.top_k(index, topk)
        k_sel = gather_rows(k, sel)
        v_sel = gather_rows(v, sel)
        s = jnp.einsum('bqhd,bqkd->bqhk', qb, k_sel).astype(jnp.float32) * scale
        valid = sel <= qpos[None, :, None]
        p = jax.nn.softmax(jnp.where(valid[:, :, None, :], s, NEG_INF), axis=-1)
        return jnp.einsum('bqhk,bqkd->bqhd', p.astype(v.dtype), v_sel)

    qpos_b = jnp.arange(seq).reshape(nblk, DSA_Q_BLOCK)
    o = lax.map(block, (to_blocks(q), to_blocks(qi), to_blocks(wi), qpos_b))
    o = o.swapaxes(0, 1).reshape(bsz, seq, h * dh)
    return o @ w_out


def _moba_mixer(x, w_in, w_out):
    bsz, seq, _ = x.shape
    h, dh, blk, qc_len = MOBA_HEADS, MOBA_HEAD_DIM, MOBA_BLOCK, MOBA_Q_CHUNK
    q, k, v = jnp.split(x @ w_in, 3, axis=-1)
    cos, sin = _rope_cos_sin(seq, dh)
    q = _apply_rope(q.reshape(bsz, seq, h, dh), cos, sin)
    k = _apply_rope(k.reshape(bsz, seq, h, dh), cos, sin)
    v = v.reshape(bsz, seq, h, dh)
    n_kb = -(-seq // blk)
    pad = n_kb * blk - seq
    padw = ((0, 0), (0, pad), (0, 0), (0, 0))
    k_blocks = jnp.pad(k, padw).reshape(bsz, n_kb, blk, h, dh).transpose(0, 3, 1, 2, 4)
    v_blocks = jnp.pad(v, padw).reshape(bsz, n_kb, blk, h, dh).transpose(0, 3, 1, 2, 4)
    k_mean = jnp.mean(k_blocks.astype(jnp.float32), axis=3)
    n_sel = min(MOBA_TOPK, n_kb - 1)
    n_chunk = seq // qc_len
    scale = dh ** -0.5
    b_ix = jnp.arange(bsz)[:, None, None, None]
    h_ix = jnp.arange(h)[None, None, :, None]
    blk_ids = jnp.arange(n_kb)

    def chunk(args):
        qc, c = args
        start = c * qc_len
        own = start // blk
        qpos = start + jnp.arange(qc_len)
        k_own = lax.dynamic_index_in_dim(k_blocks, own, axis=2, keepdims=False)
        v_own = lax.dynamic_index_in_dim(v_blocks, own, axis=2, keepdims=False)
        own_pos = own * blk + jnp.arange(blk)
        s_own = jnp.einsum('bqhd,bhsd->bhqs', qc, k_own).astype(jnp.float32) * scale
        s_own = jnp.where((own_pos[None, :] <= qpos[:, None])[None, None], s_own, NEG_INF)
        if n_sel == 0:
            p = jax.nn.softmax(s_own, axis=-1).astype(v.dtype)
            return jnp.einsum('bhqs,bhsd->bqhd', p, v_own)
        gate = jnp.einsum('bqhd,bhnd->bqhn', qc.astype(jnp.float32), k_mean)
        gate = jnp.where(blk_ids < own, gate, NEG_INF)
        _, sel = lax.top_k(gate, n_sel)
        valid = sel < own
        k_sel = k_blocks[b_ix, h_ix, sel]
        v_sel = v_blocks[b_ix, h_ix, sel]
        s_sel = jnp.einsum('bqhd,bqhjsd->bhqjs', qc, k_sel).astype(jnp.float32) * scale
        s_sel = jnp.where(valid.transpose(0, 2, 1, 3)[..., None], s_sel, NEG_INF)
        scores = jnp.concatenate([s_sel.reshape(bsz, h, qc_len, n_sel * blk), s_own], axis=-1)
        p = jax.nn.softmax(scores, axis=-1).astype(v.dtype)
        p_sel = p[..., :n_sel * blk].reshape(bsz, h, qc_len, n_sel, blk)
        p_own = p[..., n_sel * blk:]
        return (jnp.einsum('bhqjs,bqhjsd->bqhd', p_sel, v_sel)
                + jnp.einsum('bhqs,bhsd->bqhd', p_own, v_own))

    q_chunks = q.reshape(bsz, n_chunk, qc_len, h, dh).swapaxes(0, 1)
    o = lax.map(chunk, (q_chunks, jnp.arange(n_chunk)))
    o = o.swapaxes(0, 1).reshape(bsz, seq, h * dh)
    return o @ w_out


def _retention_mixer(x, w_in, gn_g, gn_b, w_out):
    bsz, seq, _ = x.shape
    h, dk, dv, C = RET_HEADS, RET_QK_DIM, RET_V_DIM, RET_CHUNK
    q, k, v, g = jnp.split(x @ w_in, [h * dk, 2 * h * dk, 2 * h * dk + h * dv], axis=-1)
    cos, sin = _rope_cos_sin(seq, dk)
    q = _apply_rope(q.reshape(bsz, seq, h, dk), cos, sin)
    k = _apply_rope(k.reshape(bsz, seq, h, dk), cos, sin) * (dk ** -0.5)
    v = v.reshape(bsz, seq, h, dv)
    log_gamma = jnp.log1p(-(2.0 ** (-5.0 - jnp.arange(h, dtype=jnp.float32))))
    pos = jnp.arange(C, dtype=jnp.float32)
    rel = pos[:, None] - pos[None, :]
    decay = jnp.where(rel[None] >= 0,
                      jnp.exp(jnp.maximum(rel, 0.0)[None] * log_gamma[:, None, None]), 0.0)
    q_decay = jnp.exp((pos + 1.0)[None, :] * log_gamma[:, None])
    k_decay = jnp.exp((C - 1.0 - pos)[None, :] * log_gamma[:, None])
    chunk_decay = jnp.exp(C * log_gamma)
    n_c = seq // C

    def to_chunks(t):
        return t.astype(jnp.float32).reshape(bsz, n_c, C, h, t.shape[-1]).transpose(1, 0, 3, 2, 4)

    def step(state, inp):
        qc, kc, vc = inp
        inner = jnp.einsum('bhij,bhje->bhie', jnp.einsum('bhid,bhjd->bhij', qc, kc) * decay, vc)
        cross = jnp.einsum('bhid,bhde->bhie', qc, state) * q_decay[None, :, :, None]
        state = (state * chunk_decay[None, :, None, None]
                 + jnp.einsum('bhjd,bhje->bhde', kc, vc * k_decay[None, :, :, None]))
        return state, inner + cross

    state0 = jnp.zeros((bsz, h, dk, dv), jnp.float32)
    _, o = lax.scan(step, state0, (to_chunks(q), to_chunks(k), to_chunks(v)))
    o = o.transpose(1, 0, 3, 2, 4).reshape(bsz, seq, h, dv)
    mu = jnp.mean(o, axis=-1, keepdims=True)
    var = jnp.mean(jnp.square(o - mu), axis=-1, keepdims=True)
    o = ((o - mu) * lax.rsqrt(var + GN_EPS)).reshape(bsz, seq, h * dv) * gn_g + gn_b
    y = (jax.nn.silu(g.astype(jnp.float32)) * o).astype(x.dtype)
    return y @ w_out


def _conv_ffn(x, w_up, conv_w, conv_b, w_down):
    hdn = x @ w_up
    ch = hdn.shape[-1]
    hdn = lax.conv_general_dilated(
        hdn, conv_w[:, None, :].astype(hdn.dtype), window_strides=(1,),
        padding=[(CONV_WIDTH - 1, 0)], dimension_numbers=('NWC', 'WIO', 'NWC'),
        feature_group_count=ch) + conv_b
    gate, up = jnp.split(hdn, 2, axis=-1)
    return (jax.nn.silu(gate) * up) @ w_down


def _normal(key, shape, scale):
    return jax.random.normal(key, shape, jnp.float32) * scale


def _layer_params(key, i, kind):
    ks = jax.random.split(key, 12)
    p = 'l%d_' % i
    if kind == 'dsa':
        widths = (DSA_HEADS * DSA_HEAD_DIM, DSA_HEAD_DIM, DSA_HEAD_DIM, IDX_HEADS * IDX_DIM, IDX_DIM, IDX_HEADS)
        out_in = DSA_HEADS * DSA_HEAD_DIM
    elif kind == 'moba':
        widths = (MOBA_HEADS * MOBA_HEAD_DIM,) * 3
        out_in = MOBA_HEADS * MOBA_HEAD_DIM
    else:
        widths = (RET_HEADS * RET_QK_DIM, RET_HEADS * RET_QK_DIM, RET_HEADS * RET_V_DIM, RET_HEADS * RET_V_DIM)
        out_in = RET_HEADS * RET_V_DIM
    col_scale = jnp.concatenate([jnp.full((w,), BETA if j == 2 else 1.0, jnp.float32)
                                 for j, w in enumerate(widths)])
    d = {}
    d[p + 'w_in'] = _normal(ks[0], (D_MODEL, sum(widths)), D_MODEL ** -0.5) * col_scale
    if kind == 'retention':
        d[p + 'gn_g'] = 1.0 + _normal(ks[1], (RET_HEADS * RET_V_DIM,), 0.02)
        d[p + 'gn_b'] = _normal(ks[2], (RET_HEADS * RET_V_DIM,), 0.02)
    d[p + 'w_out'] = _normal(ks[3], (out_in, D_MODEL), out_in ** -0.5 * BETA)
    d[p + 'ln1_g'] = 1.0 + _normal(ks[4], (D_MODEL,), 0.02)
    d[p + 'ln1_b'] = _normal(ks[5], (D_MODEL,), 0.02)
    d[p + 'w_up'] = _normal(ks[6], (D_MODEL, 2 * D_FF), D_MODEL ** -0.5 * BETA)
    d[p + 'conv_w'] = _normal(ks[7], (CONV_WIDTH, 2 * D_FF), CONV_WIDTH ** -0.5)
    d[p + 'conv_b'] = _normal(ks[8], (2 * D_FF,), 0.02)
    d[p + 'w_down'] = _normal(ks[9], (D_FF, D_MODEL), D_FF ** -0.5 * BETA)
    d[p + 'ln2_g'] = 1.0 + _normal(ks[10], (D_MODEL,), 0.02)
    d[p + 'ln2_b'] = _normal(ks[11], (D_MODEL,), 0.02)
    return d


def setup_inputs(seed: int = 0) -> dict:
    key = jax.random.key(seed)
    keys = jax.random.split(key, 1 + DEPTH)
    params = {'x': jax.random.normal(keys[0], (BATCH, SEQ, D_MODEL), jnp.float32)}
    for i in range(DEPTH):
        params.update(_layer_params(keys[1 + i], i, MIXER_ORDER[i % N_MIXERS]))
    return params


def reference(x,
              l0_w_in, l0_w_out, l0_ln1_g, l0_ln1_b, l0_w_up, l0_conv_w, l0_conv_b, l0_w_down, l0_ln2_g, l0_ln2_b,
              l1_w_in, l1_w_out, l1_ln1_g, l1_ln1_b, l1_w_up, l1_conv_w, l1_conv_b, l1_w_down, l1_ln2_g, l1_ln2_b,
              l2_w_in, l2_gn_g, l2_gn_b, l2_w_out, l2_ln1_g, l2_ln1_b, l2_w_up, l2_conv_w, l2_conv_b, l2_w_down, l2_ln2_g, l2_ln2_b,
              l3_w_in, l3_w_out, l3_ln1_g, l3_ln1_b, l3_w_up, l3_conv_w, l3_conv_b, l3_w_down, l3_ln2_g, l3_ln2_b):
    layers = (
        ((l0_w_in, l0_w_out), (l0_ln1_g, l0_ln1_b), (l0_w_up, l0_conv_w, l0_conv_b, l0_w_down), (l0_ln2_g, l0_ln2_b)),
        ((l1_w_in, l1_w_out), (l1_ln1_g, l1_ln1_b), (l1_w_up, l1_conv_w, l1_conv_b, l1_w_down), (l1_ln2_g, l1_ln2_b)),
        ((l2_w_in, l2_gn_g, l2_gn_b, l2_w_out), (l2_ln1_g, l2_ln1_b), (l2_w_up, l2_conv_w, l2_conv_b, l2_w_down), (l2_ln2_g, l2_ln2_b)),
        ((l3_w_in, l3_w_out), (l3_ln1_g, l3_ln1_b), (l3_w_up, l3_conv_w, l3_conv_b, l3_w_down), (l3_ln2_g, l3_ln2_b)),
    )
    mixer_fns = {'dsa': _dsa_mixer, 'moba': _moba_mixer, 'retention': _retention_mixer}
    for i in range(DEPTH):
        mix_p, ln1, ffn_p, ln2 = layers[i]
        mixer = mixer_fns[MIXER_ORDER[i % N_MIXERS]]
        x = _layer_norm(ALPHA * x + mixer(x, *mix_p), *ln1)
        x = _layer_norm(ALPHA * x + _conv_ffn(x, *ffn_p), *ln2)
    return x
```

```python
import functools

import jax
import jax.numpy as jnp
import numpy as np
from jax import lax
from jax.experimental import pallas as pl
from jax.experimental.pallas import tpu as pltpu

F32 = jnp.float32
BF16 = jnp.bfloat16
I32 = jnp.int32

LANES = 128
VMEM_LIMIT_BYTES = 56 * 1024 * 1024

D_MODEL = 1024
DEPTH = 4
ROPE_THETA = 10000.0
ALPHA = (2 * DEPTH) ** 0.25
LN_EPS = 1e-5
GN_EPS = 1e-6
NEG = -1e30
INT_MIN = -(2 ** 31)

DSA_HEADS = 16
DSA_HEAD_DIM = 64
IDX_HEADS = 8
IDX_DIM = 64
DSA_TOPK = 256
DSA_Q = 128
DSA_TK = 256

MOBA_HEADS = 16
MOBA_HEAD_DIM = 64
MOBA_BLOCK = 256
MOBA_TOPK = 3

RET_HEADS = 4
RET_QK_DIM = 256
RET_V_DIM = 512
RET_CHUNK = 256

D_FF = 2816
FFN_CHUNK = 256
ROW_TILE = 512
HALO = 16


def _params(*sem):
    return pltpu.CompilerParams(dimension_semantics=sem, vmem_limit_bytes=VMEM_LIMIT_BYTES)


def _nt_dot(a, b):
    return lax.dot_general(a, b, (((1,), (1,)), ((), ())), preferred_element_type=F32)


def _rope_tables(seq, dim):
    half = dim // 2
    inv = 1.0 / (ROPE_THETA ** (jnp.arange(0, dim, 2, dtype=F32) / dim))
    ang = jnp.arange(seq, dtype=F32)[:, None] * inv[None, :]
    cos, sin = jnp.cos(ang), jnp.sin(ang)
    if half >= LANES:
        return cos, sin
    reps = LANES // dim
    cos_t = jnp.tile(jnp.concatenate([cos, cos], axis=1), (1, reps))
    sin_t = jnp.tile(jnp.concatenate([-sin, sin], axis=1), (1, reps))
    return cos_t, sin_t


def _layer_norm_rows(y, g, b):
    mu = jnp.mean(y, axis=-1, keepdims=True)
    d = y - mu
    var = jnp.mean(d * d, axis=-1, keepdims=True)
    return d * lax.rsqrt(var + LN_EPS) * g + b


def _proj_body(x_ref, w_ref, cos_ref, sin_ref, *out_refs, chunks, block_mean):
    o_ref = out_refs[0]
    xb = x_ref[...].astype(BF16)
    tm = xb.shape[0]
    cos = cos_ref[...]
    sin = sin_ref[...]
    lane = lax.broadcasted_iota(I32, (tm, LANES), 1)
    first_half = (lane % 64) < 32
    for c0, width, kind, scale in chunks:
        y = jnp.dot(xb, w_ref[:, c0:c0 + width], preferred_element_type=F32)
        if kind == "rope64":
            parts = []
            for g in range(width // LANES):
                yg = y[:, g * LANES:(g + 1) * LANES]
                rot = jnp.where(first_half, pltpu.roll(yg, LANES - 32, 1), pltpu.roll(yg, 32, 1))
                parts.append(yg * cos + rot * sin)
            y = jnp.concatenate(parts, axis=1) if len(parts) > 1 else parts[0]
        elif kind == "rope256":
            parts = []
            for g in range(width // 256):
                x1 = y[:, g * 256:g * 256 + LANES]
                x2 = y[:, g * 256 + LANES:(g + 1) * 256]
                parts += [x1 * cos - x2 * sin, x1 * sin + x2 * cos]
            y = jnp.concatenate(parts, axis=1)
        if scale != 1.0:
            y = y * scale
        o_ref[:, c0:c0 + width] = y.astype(o_ref.dtype)
        if block_mean is not None and block_mean[0] <= c0 < block_mean[1]:
            km_ref = out_refs[1]
            blk = block_mean[2]
            for r in range(tm // blk):
                km_ref[r, :, c0 - block_mean[0]:c0 - block_mean[0] + width] = jnp.mean(
                    y[r * blk:(r + 1) * blk], axis=0, keepdims=True)


def _proj(x2d, w, cos, sin, seq, chunks, out_dtype, block_mean=None):
    t, d = x2d.shape
    n = w.shape[1]
    tm = ROW_TILE
    tiles_per_seq = seq // tm
    out_shape = [jax.ShapeDtypeStruct((t, n), out_dtype)]
    out_specs = [pl.BlockSpec((tm, n), lambda i: (i, 0))]
    if block_mean is not None:
        lo, hi, blk = block_mean
        out_shape.append(jax.ShapeDtypeStruct((t // blk, 1, hi - lo), F32))
        out_specs.append(pl.BlockSpec((tm // blk, 1, hi - lo), lambda i: (i, 0, 0)))
    res = pl.pallas_call(
        functools.partial(_proj_body, chunks=tuple(chunks), block_mean=block_mean),
        grid=(t // tm,),
        in_specs=[
            pl.BlockSpec((tm, d), lambda i: (i, 0)),
            pl.BlockSpec((d, n), lambda i: (0, 0)),
            pl.BlockSpec((tm, LANES), lambda i: (i % tiles_per_seq, 0)),
            pl.BlockSpec((tm, LANES), lambda i: (i % tiles_per_seq, 0)),
        ],
        out_specs=out_specs,
        out_shape=out_shape,
        compiler_params=_params("parallel"),
    )(x2d, w, cos, sin)
    return res if block_mean is not None else res[0]


def _out_ln_body(a_ref, w_ref, x_ref, g_ref, b_ref, o_ref):
    f = jnp.dot(a_ref[...], w_ref[...], preferred_element_type=F32)
    y = ALPHA * x_ref[...] + f
    o_ref[...] = _layer_norm_rows(y, g_ref[...], b_ref[...])


def _out_ln(a, w, x2d, g, b):
    t, kin = a.shape
    d = x2d.shape[1]
    tm = ROW_TILE
    return pl.pallas_call(
        _out_ln_body,
        grid=(t // tm,),
        in_specs=[
            pl.BlockSpec((tm, kin), lambda i: (i, 0)),
            pl.BlockSpec((kin, d), lambda i: (0, 0)),
            pl.BlockSpec((tm, d), lambda i: (i, 0)),
            pl.BlockSpec((1, d), lambda i: (0, 0)),
            pl.BlockSpec((1, d), lambda i: (0, 0)),
        ],
        out_specs=pl.BlockSpec((tm, d), lambda i: (i, 0)),
        out_shape=jax.ShapeDtypeStruct((t, d), F32),
        compiler_params=_params("parallel"),
    )(a, w, x2d, g.reshape(1, d), b.reshape(1, d))


def _ffn_body(x_ref, halo_ref, wup_ref, cw_ref, cb_ref, wdn_ref, g_ref, b_ref, o_ref,
              xe_ref, h_ref, acc_ref, *, tiles_per_seq):
    i = pl.program_id(0)
    tm = x_ref.shape[0]
    seq_start = (i % tiles_per_seq) == 0
    halo = halo_ref[...]
    xe_ref[0:HALO, :] = jnp.where(seq_start, 0.0, halo).astype(BF16)
    xe_ref[HALO:, :] = x_ref[...].astype(BF16)
    xe = xe_ref[...]

    def conv(c0):
        h_ref[...] = jnp.dot(xe, wup_ref[:, c0:c0 + FFN_CHUNK], preferred_element_type=F32)
        w = cw_ref[:, c0:c0 + FFN_CHUNK]
        return (h_ref[HALO - 2:HALO - 2 + tm, :] * w[0:1, :]
                + h_ref[HALO - 1:HALO - 1 + tm, :] * w[1:2, :]
                + h_ref[HALO:HALO + tm, :] * w[2:3, :]
                + cb_ref[:, c0:c0 + FFN_CHUNK])

    for c in range(D_FF // FFN_CHUNK):
        c0 = c * FFN_CHUNK
        gate = conv(c0)
        up = conv(D_FF + c0)
        act = (gate * jax.nn.sigmoid(gate) * up).astype(BF16)
        part = jnp.dot(act, wdn_ref[c0:c0 + FFN_CHUNK, :], preferred_element_type=F32)
        if c == 0:
            acc_ref[...] = part
        else:
            acc_ref[...] += part
    y = ALPHA * x_ref[...] + acc_ref[...]
    o_ref[...] = _layer_norm_rows(y, g_ref[...], b_ref[...])


def _ffn_ln(x2d, seq, w_up, conv_w, conv_b, w_down, g, b):
    t, d = x2d.shape
    tm = ROW_TILE
    tiles_per_seq = seq // tm
    halo_blocks = tm // HALO
    return pl.pallas_call(
        functools.partial(_ffn_body, tiles_per_seq=tiles_per_seq),
        grid=(t // tm,),
        in_specs=[
            pl.BlockSpec((tm, d), lambda i: (i, 0)),
            pl.BlockSpec((HALO, d), lambda i: (jnp.maximum(i * halo_blocks - 1, 0), 0)),
            pl.BlockSpec((d, 2 * D_FF), lambda i: (0, 0)),
            pl.BlockSpec((3, 2 * D_FF), lambda i: (0, 0)),
            pl.BlockSpec((1, 2 * D_FF), lambda i: (0, 0)),
            pl.BlockSpec((D_FF, d), lambda i: (0, 0)),
            pl.BlockSpec((1, d), lambda i: (0, 0)),
            pl.BlockSpec((1, d), lambda i: (0, 0)),
        ],
        out_specs=pl.BlockSpec((tm, d), lambda i: (i, 0)),
        out_shape=jax.ShapeDtypeStruct((t, d), F32),
        scratch_shapes=[
            pltpu.VMEM((tm + HALO, d), BF16),
            pltpu.VMEM((tm + HALO, FFN_CHUNK), F32),
            pltpu.VMEM((tm, d), F32),
        ],
        compiler_params=_params("parallel"),
    )(x2d, x2d, w_up.astype(BF16), conv_w, conv_b.reshape(1, -1), w_down.astype(BF16),
      g.reshape(1, d), b.reshape(1, d))


def _sortable_key(score):
    bits = lax.bitcast_convert_type(score, I32)
    key = jnp.where(bits < 0, bits ^ jnp.int32(0x7FFFFFFF), bits)
    return jnp.where(score == 0.0, 0, key)


def _dsa_body(q_ref, qi_ref, wi_ref, k_ref, ki_ref, v_ref, o_ref,
              key_ref, lhs_ref, wb_ref, thr_ref, m_ref, l_ref, acc_ref, *, topk):
    qn, tk = DSA_Q, DSA_TK
    groups = tk // LANES
    i = pl.program_id(1)
    nkt = (i * qn + qn + tk - 1) // tk
    lane = lax.broadcasted_iota(I32, (qn, LANES), 1)
    low = lane < 64
    row_pos = i * qn + lax.broadcasted_iota(I32, (qn, LANES), 0)

    for p in range(IDX_HEADS // 2):
        pair = qi_ref[:, p * LANES:(p + 1) * LANES]
        lhs_ref[2 * p] = jnp.where(low, pair, jnp.zeros_like(pair))
        lhs_ref[2 * p + 1] = jnp.where(low, jnp.zeros_like(pair), pair)
    for h in range(IDX_HEADS):
        wb_ref[h] = jnp.broadcast_to(wi_ref[:, h:h + 1], (qn, LANES))

    def index_tile(kt, carry):
        ki_t = ki_ref[pl.ds(pl.multiple_of(kt * tk, tk), tk), :]
        acc = [jnp.zeros((qn, LANES), F32) for _ in range(groups)]
        for h in range(IDX_HEADS):
            lg = _nt_dot(lhs_ref[h], ki_t)
            wb = wb_ref[h]
            for g in range(groups):
                acc[g] = acc[g] + wb * jnp.maximum(lg[:, g * LANES:(g + 1) * LANES], 0.0)
        for g in range(groups):
            col = kt * tk + g * LANES + lane
            key_ref[kt, :, g * LANES:(g + 1) * LANES] = jnp.where(
                col <= row_pos, _sortable_key(acc[g]), INT_MIN)
        return carry

    lax.fori_loop(0, nkt, index_tile, 0)

    def count(pred):
        def body(kt, acc):
            for g in range(groups):
                keys = key_ref[kt, :, g * LANES:(g + 1) * LANES]
                acc = acc + jnp.where(pred(keys, kt * tk + g * LANES + lane), 1.0, 0.0)
            return acc
        acc = lax.fori_loop(0, nkt, body, jnp.zeros((qn, LANES), F32))
        return jnp.sum(acc, axis=-1, keepdims=True)

    def bit_step(it, t_u):
        cand_u = t_u | jnp.left_shift(jnp.int32(1), 31 - it)
        cand = jnp.broadcast_to(cand_u ^ INT_MIN, (qn, LANES))
        cnt = count(lambda keys, pos: keys >= cand)
        return jnp.where(cnt >= topk, cand_u, t_u)

    t_u = lax.fori_loop(0, 32, bit_step, jnp.zeros((qn, 1), I32))
    thr1 = jnp.maximum(t_u ^ INT_MIN, INT_MIN + 1)
    thr = jnp.broadcast_to(thr1, (qn, LANES))
    thr_ref[...] = thr
    cnt_gt = count(lambda keys, pos: keys > thr)
    cnt_eq = count(lambda keys, pos: keys == thr)
    need = topk - cnt_gt
    excess_ties = jnp.max(jnp.where(cnt_eq > need, 1.0, 0.0)) > 0.0

    @pl.when(excess_ties)
    def _():
        need_b = jnp.broadcast_to(need, (qn, 1))
        pos_bits = max(1, int(np.ceil(np.log2(key_ref.shape[0] * tk))))

        def pos_step(it, p_lim):
            cand1 = p_lim | jnp.left_shift(jnp.int32(1), pos_bits - 1 - it)
            cand = jnp.broadcast_to(cand1, (qn, LANES))
            cnt = count(lambda keys, pos: (keys == thr) & (pos < cand))
            return jnp.where(cnt < need_b, cand1, p_lim)

        p_lim = jnp.broadcast_to(
            lax.fori_loop(0, pos_bits, pos_step, jnp.zeros((qn, 1), I32)), (qn, LANES))

        def demote(kt, carry):
            for g in range(groups):
                keys = key_ref[kt, :, g * LANES:(g + 1) * LANES]
                pos = kt * tk + g * LANES + lane
                key_ref[kt, :, g * LANES:(g + 1) * LANES] = jnp.where(
                    (keys == thr) & (pos > p_lim), thr - 1, keys)
            return carry

        lax.fori_loop(0, nkt, demote, 0)

    m_ref[...] = jnp.full(m_ref.shape, NEG, F32)
    l_ref[...] = jnp.zeros(l_ref.shape, F32)
    acc_ref[...] = jnp.zeros(acc_ref.shape, F32)

    def attend_tile(kt, carry):
        start = pl.multiple_of(kt * tk, tk)
        k_t = k_ref[pl.ds(start, tk), :]
        v_t = v_ref[pl.ds(start, tk), :]
        thr_t = thr_ref[...]
        sel = [key_ref[kt, :, g * LANES:(g + 1) * LANES] >= thr_t for g in range(groups)]
        for h in range(DSA_HEADS):
            pair = q_ref[:, (h // 2) * LANES:(h // 2 + 1) * LANES]
            zero = jnp.zeros_like(pair)
            qh = jnp.where(low, pair, zero) if h % 2 == 0 else jnp.where(low, zero, pair)
            s = _nt_dot(qh, k_t)
            sg = [jnp.where(sel[g], s[:, g * LANES:(g + 1) * LANES], NEG) for g in range(groups)]
            mt = functools.reduce(jnp.maximum, sg)
            m_old = m_ref[h]
            m_new = jnp.maximum(m_old, jnp.max(mt, axis=-1, keepdims=True))
            alpha = jnp.exp(m_old - m_new)
            pg = [jnp.exp(x - m_new) for x in sg]
            l_ref[h] = alpha * l_ref[h] + jnp.sum(functools.reduce(jnp.add, pg), axis=-1, keepdims=True)
            m_ref[h] = m_new
            p = jnp.concatenate(pg, axis=1).astype(BF16)
            acc_ref[h] = alpha * acc_ref[h] + jnp.dot(p, v_t, preferred_element_type=F32)
        return carry

    lax.fori_loop(0, nkt, attend_tile, 0)

    for p in range(DSA_HEADS // 2):
        o0 = acc_ref[2 * p] / l_ref[2 * p]
        o1 = acc_ref[2 * p + 1] / l_ref[2 * p + 1]
        o_ref[:, p * LANES:(p + 1) * LANES] = jnp.where(low, o0, o1).astype(o_ref.dtype)


def _dsa_attention(proj, wi, bsz, seq, topk):
    t = bsz * seq
    qn, tk = DSA_Q, DSA_TK
    nqb = seq // qn
    nkt_max = seq // tk
    qcols = DSA_HEADS * DSA_HEAD_DIM
    icols = IDX_HEADS * IDX_DIM
    kcol = (qcols + icols) // LANES
    return pl.pallas_call(
        functools.partial(_dsa_body, topk=topk),
        grid=(bsz, nqb),
        in_specs=[
            pl.BlockSpec((qn, qcols), lambda b, i: (b * nqb + i, 0)),
            pl.BlockSpec((qn, icols), lambda b, i: (b * nqb + i, qcols // icols)),
            pl.BlockSpec((qn, LANES), lambda b, i: (b * nqb + i, 0)),
            pl.BlockSpec((seq, LANES), lambda b, i: (b, kcol)),
            pl.BlockSpec((seq, LANES), lambda b, i: (b, kcol + 1)),
            pl.BlockSpec((seq, LANES), lambda b, i: (b, kcol + 2)),
        ],
        out_specs=pl.BlockSpec((qn, qcols), lambda b, i: (b * nqb + i, 0)),
        out_shape=jax.ShapeDtypeStruct((t, qcols), BF16),
        scratch_shapes=[
            pltpu.VMEM((nkt_max, qn, tk), I32),
            pltpu.VMEM((IDX_HEADS, qn, LANES), BF16),
            pltpu.VMEM((IDX_HEADS, qn, LANES), F32),
            pltpu.VMEM((qn, LANES), I32),
            pltpu.VMEM((DSA_HEADS, qn, LANES), F32),
            pltpu.VMEM((DSA_HEADS, qn, LANES), F32),
            pltpu.VMEM((DSA_HEADS, qn, LANES), F32),
        ],
        compiler_params=_params("parallel", "arbitrary"),
    )(proj, proj, wi, proj, proj, proj)


def _dsa_layer(x2d, bsz, seq, w_in, w_out, ln_g, ln_b):
    h, dh, hi, di = DSA_HEADS, DSA_HEAD_DIM, IDX_HEADS, IDX_DIM
    offs = np.cumsum([h * dh, dh, dh, hi * di, di, hi])
    wq, wk, wv, wqi, wki, wwi = (w_in[:, a:b] for a, b in zip([0] + offs[:-1].tolist(), offs.tolist()))
    zpad = jnp.zeros((D_MODEL, LANES), F32)
    w_main = jnp.concatenate([wq, wqi, wk, wk, wki, wki, wv, wv, zpad], axis=1).astype(BF16)
    w_wi = jnp.concatenate([wwi, jnp.zeros((D_MODEL, LANES - hi), F32)], axis=1).astype(BF16)
    cos, sin = _rope_tables(seq, dh)
    scale = dh ** -0.5
    chunks = [(0, 512, "rope64", scale), (512, 512, "rope64", scale), (1024, 512, "rope64", 1.0),
              (1536, 256, "rope64", 1.0), (1792, 256, "plain", 1.0)]
    proj = _proj(x2d, w_main, cos, sin, seq, chunks, BF16)
    wi = _proj(x2d, w_wi, cos, sin, seq, [(0, LANES, "plain", hi ** -0.5 * di ** -0.5)], F32)
    o = _dsa_attention(proj, wi, bsz, seq, min(DSA_TOPK, seq // 4))
    return _out_ln(o, w_out.astype(BF16), x2d, ln_g, ln_b)


def _moba_body(q_ref, k_ref, v_ref, km_ref, o_ref, sel_ref, m_ref, l_ref, acc_ref, *, n_sel):
    blk = MOBA_BLOCK
    i = pl.program_id(2)
    lane = lax.broadcasted_iota(I32, (blk, LANES), 1)
    low = lane < 64
    pair = q_ref[...]
    zero = jnp.zeros_like(pair)
    km = km_ref[0].astype(BF16)
    rows = lax.broadcasted_iota(I32, (blk, blk), 0)
    cols = lax.broadcasted_iota(I32, (blk, blk), 1)
    qh = [jnp.where(low, pair, zero), jnp.where(low, zero, pair)]

    lane_f = lane.astype(F32)
    for h in range(2):
        gate = jnp.where(lane < i, _nt_dot(qh[h], km), NEG)
        chosen = jnp.zeros((blk, LANES), F32)
        for _ in range(n_sel):
            best = jnp.max(gate, axis=-1, keepdims=True)
            first = jnp.min(jnp.where(gate == best, lane_f, float(LANES)), axis=-1, keepdims=True)
            hit = lane_f == first
            chosen = jnp.where(hit, 1.0, chosen)
            gate = jnp.where(hit, -jnp.inf, gate)
        sel_ref[h] = jnp.where(lane < i, chosen, 0.0)

        start = pl.multiple_of(i * blk, blk)
        s = jnp.where(cols <= rows, _nt_dot(qh[h], k_ref[pl.ds(start, blk), :]), NEG)
        m0 = jnp.max(s, axis=-1, keepdims=True)
        p = jnp.exp(s - m0)
        m_ref[h] = jnp.broadcast_to(m0, (blk, LANES))
        l_ref[h] = jnp.broadcast_to(jnp.sum(p, axis=-1, keepdims=True), (blk, LANES))
        acc_ref[h] = jnp.dot(p.astype(BF16), v_ref[pl.ds(start, blk), :], preferred_element_type=F32)

    def past_block(j, carry):
        start = pl.multiple_of(j * blk, blk)
        k_t = k_ref[pl.ds(start, blk), :]
        v_t = v_ref[pl.ds(start, blk), :]
        for h in range(2):
            flag = jnp.max(jnp.where(lane == j, sel_ref[h], 0.0), axis=-1, keepdims=True)
            s = jnp.where(flag > 0.0, _nt_dot(qh[h], k_t), NEG)
            m_old = m_ref[h]
            m_new = jnp.maximum(m_old, jnp.max(s, axis=-1, keepdims=True))
            alpha = jnp.exp(m_old - m_new)
            p = jnp.exp(s - m_new[:, 0:1])
            l_ref[h] = alpha * l_ref[h] + jnp.sum(p, axis=-1, keepdims=True)
            m_ref[h] = m_new
            acc_ref[h] = alpha * acc_ref[h] + jnp.dot(p.astype(BF16), v_t, preferred_element_type=F32)
        return carry

    lax.fori_loop(0, i, past_block, 0)
    o_ref[...] = jnp.where(low, acc_ref[0] / l_ref[0], acc_ref[1] / l_ref[1]).astype(o_ref.dtype)


def _moba_attention(proj, kmean, bsz, seq):
    t = bsz * seq
    blk = MOBA_BLOCK
    nqb = seq // blk
    pairs = MOBA_HEADS // 2
    n_sel = min(MOBA_TOPK, nqb - 1)
    return pl.pallas_call(
        functools.partial(_moba_body, n_sel=n_sel),
        grid=(bsz, pairs, nqb),
        in_specs=[
            pl.BlockSpec((blk, LANES), lambda b, p, i: (b * nqb + i, p)),
            pl.BlockSpec((seq, LANES), lambda b, p, i: (b, pairs + p)),
            pl.BlockSpec((seq, LANES), lambda b, p, i: (b, 2 * pairs + p)),
            pl.BlockSpec((1, LANES, LANES), lambda b, p, i: (b, 0, p)),
        ],
        out_specs=pl.BlockSpec((blk, LANES), lambda b, p, i: (b * nqb + i, p)),
        out_shape=jax.ShapeDtypeStruct((t, MOBA_HEADS * MOBA_HEAD_DIM), BF16),
        scratch_shapes=[
            pltpu.VMEM((2, blk, LANES), F32),
            pltpu.VMEM((2, blk, LANES), F32),
            pltpu.VMEM((2, blk, LANES), F32),
            pltpu.VMEM((2, blk, LANES), F32),
        ],
        compiler_params=_params("parallel", "parallel", "arbitrary"),
    )(proj, proj, proj, kmean)


def _moba_layer(x2d, bsz, seq, w_in, w_out, ln_g, ln_b):
    hd = MOBA_HEADS * MOBA_HEAD_DIM
    cos, sin = _rope_tables(seq, MOBA_HEAD_DIM)
    scale = MOBA_HEAD_DIM ** -0.5
    chunks = [(c0, 512, "rope64", scale if c0 < hd else 1.0) for c0 in range(0, 2 * hd, 512)]
    chunks += [(c0, 512, "plain", 1.0) for c0 in range(2 * hd, 3 * hd, 512)]
    proj, kmean = _proj(x2d, w_in.astype(BF16), cos, sin, seq, chunks, BF16,
                        block_mean=(hd, 2 * hd, MOBA_BLOCK))
    nb = seq // MOBA_BLOCK
    kmean = jnp.pad(kmean.reshape(bsz, nb, hd), ((0, 0), (0, LANES - nb), (0, 0)))
    o = _moba_attention(proj, kmean, bsz, seq)
    return _out_ln(o, w_out.astype(BF16), x2d, ln_g, ln_b)


def _ret_body(q_ref, k_ref, v_ref, g_ref, lg_ref, gng_ref, gnb_ref, o_ref, state_ref):
    c = pl.program_id(2)
    cc = RET_CHUNK

    @pl.when(c == 0)
    def _():
        state_ref[...] = jnp.zeros(state_ref.shape, F32)

    lg = lg_ref[0]
    ri = lax.broadcasted_iota(I32, (cc, cc), 0)
    ci = lax.broadcasted_iota(I32, (cc, cc), 1)
    rel = (ri - ci).astype(F32)
    lg_row = jnp.concatenate([lg] * (cc // LANES), axis=1)
    decay = jnp.where(rel >= 0, jnp.exp(jnp.maximum(rel, 0.0) * lg_row), 0.0)
    pos = lax.broadcasted_iota(I32, (cc, LANES), 0).astype(F32)
    q_decay = jnp.exp((pos + 1.0) * lg)[:, 0:1]
    k_decay = jnp.exp((cc - 1.0 - pos) * lg)[:, 0:1]
    chunk_decay = jnp.exp(cc * lg)[:, 0:1]

    q = q_ref[...]
    k = k_ref[...]
    v = v_ref[...]
    state = state_ref[...]
    inner = jnp.dot((_nt_dot(q, k) * decay).astype(BF16), v, preferred_element_type=F32)
    cross = jnp.dot(q, state.astype(BF16), preferred_element_type=F32) * q_decay
    kd_t = (k.astype(F32) * k_decay).T.astype(BF16)
    state_ref[...] = state * chunk_decay + jnp.dot(kd_t, v, preferred_element_type=F32)

    o = inner + cross
    mu = jnp.mean(o, axis=-1, keepdims=True)
    d = o - mu
    var = jnp.mean(d * d, axis=-1, keepdims=True)
    on = d * lax.rsqrt(var + GN_EPS) * gng_ref[...] + gnb_ref[...]
    gate = g_ref[...]
    o_ref[...] = (gate * jax.nn.sigmoid(gate) * on).astype(o_ref.dtype)


def _retention(proj, gate, lg, gn_g, gn_b, bsz, seq):
    t = bsz * seq
    cc = RET_CHUNK
    nc = seq // cc
    h, dk, dv = RET_HEADS, RET_QK_DIM, RET_V_DIM
    return pl.pallas_call(
        _ret_body,
        grid=(bsz, h, nc),
        in_specs=[
            pl.BlockSpec((cc, dk), lambda b, hh, c: (b * nc + c, hh)),
            pl.BlockSpec((cc, dk), lambda b, hh, c: (b * nc + c, h + hh)),
            pl.BlockSpec((cc, dv), lambda b, hh, c: (b * nc + c, (2 * h * dk) // dv + hh)),
            pl.BlockSpec((cc, dv), lambda b, hh, c: (b * nc + c, hh)),
            pl.BlockSpec((1, 1, LANES), lambda b, hh, c: (hh, 0, 0)),
            pl.BlockSpec((1, dv), lambda b, hh, c: (0, hh)),
            pl.BlockSpec((1, dv), lambda b, hh, c: (0, hh)),
        ],
        out_specs=pl.BlockSpec((cc, dv), lambda b, hh, c: (b * nc + c, hh)),
        out_shape=jax.ShapeDtypeStruct((t, h * dv), BF16),
        scratch_shapes=[pltpu.VMEM((dk, dv), F32)],
        compiler_params=_params("parallel", "parallel", "arbitrary"),
    )(proj, proj, proj, gate, lg, gn_g.reshape(1, -1), gn_b.reshape(1, -1))


def _retention_layer(x2d, bsz, seq, w_in, gn_g, gn_b, w_out, ln_g, ln_b):
    h, dk, dv = RET_HEADS, RET_QK_DIM, RET_V_DIM
    cos, sin = _rope_tables(seq, dk)
    nqk = 2 * h * dk
    chunks = [(c0, 512, "rope256", 1.0 if c0 < h * dk else dk ** -0.5) for c0 in range(0, nqk, 512)]
    chunks += [(c0, 512, "plain", 1.0) for c0 in range(nqk, nqk + h * dv, 512)]
    proj = _proj(x2d, w_in[:, :nqk + h * dv].astype(BF16), cos, sin, seq, chunks, BF16)
    gchunks = [(c0, 512, "plain", 1.0) for c0 in range(0, h * dv, 512)]
    gate = _proj(x2d, w_in[:, nqk + h * dv:].astype(BF16), cos, sin, seq, gchunks, F32)
    log_gamma = jnp.log1p(-(2.0 ** (-5.0 - jnp.arange(h, dtype=F32))))
    lg = jnp.broadcast_to(log_gamma[:, None, None], (h, 1, LANES))
    y = _retention(proj, gate, lg, gn_g, gn_b, bsz, seq)
    return _out_ln(y, w_out.astype(BF16), x2d, ln_g, ln_b)


def kernel(x, l0_w_in, l0_w_out, l0_ln1_g, l0_ln1_b, l0_w_up, l0_conv_w, l0_conv_b, l0_w_down, l0_ln2_g, l0_ln2_b, l1_w_in, l1_w_out, l1_ln1_g, l1_ln1_b, l1_w_up, l1_conv_w, l1_conv_b, l1_w_down, l1_ln2_g, l1_ln2_b, l2_w_in, l2_gn_g, l2_gn_b, l2_w_out, l2_ln1_g, l2_ln1_b, l2_w_up, l2_conv_w, l2_conv_b, l2_w_down, l2_ln2_g, l2_ln2_b, l3_w_in, l3_w_out, l3_ln1_g, l3_ln1_b, l3_w_up, l3_conv_w, l3_conv_b, l3_w_down, l3_ln2_g, l3_ln2_b):
    bsz, seq, d = x.shape
    h = x.reshape(bsz * seq, d)
    h = _dsa_layer(h, bsz, seq, l0_w_in, l0_w_out, l0_ln1_g, l0_ln1_b)
    h = _ffn_ln(h, seq, l0_w_up, l0_conv_w, l0_conv_b, l0_w_down, l0_ln2_g, l0_ln2_b)
    h = _moba_layer(h, bsz, seq, l1_w_in, l1_w_out, l1_ln1_g, l1_ln1_b)
    h = _ffn_ln(h, seq, l1_w_up, l1_conv_w, l1_conv_b, l1_w_down, l1_ln2_g, l1_ln2_b)
    h = _retention_layer(h, bsz, seq, l2_w_in, l2_gn_g, l2_gn_b, l2_w_out, l2_ln1_g, l2_ln1_b)
    h = _ffn_ln(h, seq, l2_w_up, l2_conv_w, l2_conv_b, l2_w_down, l2_ln2_g, l2_ln2_b)
    h = _dsa_layer(h, bsz, seq, l3_w_in, l3_w_out, l3_ln1_g, l3_ln1_b)
    h = _ffn_ln(h, seq, l3_w_up, l3_conv_w, l3_conv_b, l3_w_down, l3_ln2_g, l3_ln2_b)
    return h.reshape(bsz, seq, d)
```

```python
import functools

import jax
import jax.numpy as jnp
import numpy as np
from jax import lax
from jax.experimental import pallas as pl
from jax.experimental.pallas import tpu as pltpu

F32 = jnp.float32
BF16 = jnp.bfloat16
I32 = jnp.int32

LANES = 128
VMEM_LIMIT_BYTES = 56 * 1024 * 1024

D_MODEL = 1024
DEPTH = 4
ROPE_THETA = 10000.0
ALPHA = (2 * DEPTH) ** 0.25
LN_EPS = 1e-5
GN_EPS = 1e-6
NEG = -1e30
INT_MIN = -(2 ** 31)
LOG2E = 1.4426950408889634

DSA_HEADS = 16
DSA_HEAD_DIM = 64
IDX_HEADS = 8
IDX_DIM = 64
DSA_TOPK = 256
DSA_Q = 128
DSA_TK = 512

MOBA_HEADS = 16
MOBA_HEAD_DIM = 64
MOBA_BLOCK = 256
MOBA_TOPK = 3
MOBA_PAIRS_PER_STEP = 2

RET_HEADS = 4
RET_QK_DIM = 256
RET_V_DIM = 512
RET_CHUNK = 256

D_FF = 2816
FFN_CHUNK = 256
ROW_TILE = 512
HALO = 16


def _params(*sem):
    return pltpu.CompilerParams(dimension_semantics=sem, vmem_limit_bytes=VMEM_LIMIT_BYTES)


def _nt_dot(a, b):
    return lax.dot_general(a, b, (((1,), (1,)), ((), ())), preferred_element_type=F32)


def _rope_tables(seq, dim):
    half = dim // 2
    inv = 1.0 / (ROPE_THETA ** (jnp.arange(0, dim, 2, dtype=F32) / dim))
    ang = jnp.arange(seq, dtype=F32)[:, None] * inv[None, :]
    cos, sin = jnp.cos(ang), jnp.sin(ang)
    if half >= LANES:
        return cos, sin
    reps = LANES // dim
    cos_t = jnp.tile(jnp.concatenate([cos, cos], axis=1), (1, reps))
    sin_t = jnp.tile(jnp.concatenate([-sin, sin], axis=1), (1, reps))
    return cos_t, sin_t


def _layer_norm_rows(y, g, b):
    mu = jnp.mean(y, axis=-1, keepdims=True)
    d = y - mu
    var = jnp.mean(d * d, axis=-1, keepdims=True)
    return d * lax.rsqrt(var + LN_EPS) * g + b


def _proj_body(x_ref, w_ref, cos_ref, sin_ref, *out_refs, chunks, block_mean):
    o_ref = out_refs[0]
    xb = x_ref[...].astype(BF16)
    tm = xb.shape[0]
    cos = cos_ref[...]
    sin = sin_ref[...]
    lane = lax.broadcasted_iota(I32, (tm, LANES), 1)
    first_half = (lane % 64) < 32
    for c0, width, kind, scale in chunks:
        y = jnp.dot(xb, w_ref[:, c0:c0 + width], preferred_element_type=F32)
        if kind == "rope64":
            parts = []
            for g in range(width // LANES):
                yg = y[:, g * LANES:(g + 1) * LANES]
                rot = jnp.where(first_half, pltpu.roll(yg, LANES - 32, 1), pltpu.roll(yg, 32, 1))
                parts.append(yg * cos + rot * sin)
            y = jnp.concatenate(parts, axis=1) if len(parts) > 1 else parts[0]
        elif kind == "rope256":
            parts = []
            for g in range(width // 256):
                x1 = y[:, g * 256:g * 256 + LANES]
                x2 = y[:, g * 256 + LANES:(g + 1) * 256]
                parts += [x1 * cos - x2 * sin, x1 * sin + x2 * cos]
            y = jnp.concatenate(parts, axis=1)
        if scale != 1.0:
            y = y * scale
        o_ref[:, c0:c0 + width] = y.astype(o_ref.dtype)
        if block_mean is not None and block_mean[0] <= c0 < block_mean[1]:
            km_ref = out_refs[1]
            blk = block_mean[2]
            for r in range(tm // blk):
                km_ref[r, :, c0 - block_mean[0]:c0 - block_mean[0] + width] = jnp.mean(
                    y[r * blk:(r + 1) * blk], axis=0, keepdims=True)


def _proj(x2d, w, cos, sin, seq, chunks, out_dtype, block_mean=None):
    t, d = x2d.shape
    n = w.shape[1]
    tm = ROW_TILE
    tiles_per_seq = seq // tm
    out_shape = [jax.ShapeDtypeStruct((t, n), out_dtype)]
    out_specs = [pl.BlockSpec((tm, n), lambda i: (i, 0))]
    if block_mean is not None:
        lo, hi, blk = block_mean
        out_shape.append(jax.ShapeDtypeStruct((t // blk, 1, hi - lo), F32))
        out_specs.append(pl.BlockSpec((tm // blk, 1, hi - lo), lambda i: (i, 0, 0)))
    res = pl.pallas_call(
        functools.partial(_proj_body, chunks=tuple(chunks), block_mean=block_mean),
        grid=(t // tm,),
        in_specs=[
            pl.BlockSpec((tm, d), lambda i: (i, 0)),
            pl.BlockSpec((d, n), lambda i: (0, 0)),
            pl.BlockSpec((tm, LANES), lambda i: (i % tiles_per_seq, 0)),
            pl.BlockSpec((tm, LANES), lambda i: (i % tiles_per_seq, 0)),
        ],
        out_specs=out_specs,
        out_shape=out_shape,
        compiler_params=_params("parallel"),
    )(x2d, w, cos, sin)
    return res if block_mean is not None else res[0]


def _out_ln_body(a_ref, w_ref, x_ref, g_ref, b_ref, o_ref):
    f = jnp.dot(a_ref[...], w_ref[...], preferred_element_type=F32)
    y = ALPHA * x_ref[...] + f
    o_ref[...] = _layer_norm_rows(y, g_ref[...], b_ref[...])


def _out_ln(a, w, x2d, g, b):
    t, kin = a.shape
    d = x2d.shape[1]
    tm = ROW_TILE
    return pl.pallas_call(
        _out_ln_body,
        grid=(t // tm,),
        in_specs=[
            pl.BlockSpec((tm, kin), lambda i: (i, 0)),
            pl.BlockSpec((kin, d), lambda i: (0, 0)),
            pl.BlockSpec((tm, d), lambda i: (i, 0)),
            pl.BlockSpec((1, d), lambda i: (0, 0)),
            pl.BlockSpec((1, d), lambda i: (0, 0)),
        ],
        out_specs=pl.BlockSpec((tm, d), lambda i: (i, 0)),
        out_shape=jax.ShapeDtypeStruct((t, d), F32),
        compiler_params=_params("parallel"),
    )(a, w, x2d, g.reshape(1, d), b.reshape(1, d))


def _ffn_body(x_ref, halo_ref, wup_ref, cw_ref, cb_ref, wdn_ref, g_ref, b_ref, o_ref,
              xe_ref, h_ref, acc_ref, *, tiles_per_seq):
    i = pl.program_id(0)
    tm = x_ref.shape[0]
    seq_start = (i % tiles_per_seq) == 0
    halo = halo_ref[...]
    xe_ref[0:HALO, :] = jnp.where(seq_start, 0.0, halo).astype(BF16)
    xe_ref[HALO:, :] = x_ref[...].astype(BF16)
    xe = xe_ref[...]

    def conv(c0):
        h_ref[...] = jnp.dot(xe, wup_ref[:, c0:c0 + FFN_CHUNK], preferred_element_type=F32)
        w = cw_ref[:, c0:c0 + FFN_CHUNK]
        return (h_ref[HALO - 2:HALO - 2 + tm, :] * w[0:1, :]
                + h_ref[HALO - 1:HALO - 1 + tm, :] * w[1:2, :]
                + h_ref[HALO:HALO + tm, :] * w[2:3, :]
                + cb_ref[:, c0:c0 + FFN_CHUNK])

    for c in range(D_FF // FFN_CHUNK):
        c0 = c * FFN_CHUNK
        gate = conv(c0)
        up = conv(D_FF + c0)
        act = (gate * jax.nn.sigmoid(gate) * up).astype(BF16)
        part = jnp.dot(act, wdn_ref[c0:c0 + FFN_CHUNK, :], preferred_element_type=F32)
        if c == 0:
            acc_ref[...] = part
        else:
            acc_ref[...] += part
    y = ALPHA * x_ref[...] + acc_ref[...]
    o_ref[...] = _layer_norm_rows(y, g_ref[...], b_ref[...])


def _ffn_ln(x2d, seq, w_up, conv_w, conv_b, w_down, g, b):
    t, d = x2d.shape
    tm = ROW_TILE
    tiles_per_seq = seq // tm
    halo_blocks = tm // HALO
    return pl.pallas_call(
        functools.partial(_ffn_body, tiles_per_seq=tiles_per_seq),
        grid=(t // tm,),
        in_specs=[
            pl.BlockSpec((tm, d), lambda i: (i, 0)),
            pl.BlockSpec((HALO, d), lambda i: (jnp.maximum(i * halo_blocks - 1, 0), 0)),
            pl.BlockSpec((d, 2 * D_FF), lambda i: (0, 0)),
            pl.BlockSpec((3, 2 * D_FF), lambda i: (0, 0)),
            pl.BlockSpec((1, 2 * D_FF), lambda i: (0, 0)),
            pl.BlockSpec((D_FF, d), lambda i: (0, 0)),
            pl.BlockSpec((1, d), lambda i: (0, 0)),
            pl.BlockSpec((1, d), lambda i: (0, 0)),
        ],
        out_specs=pl.BlockSpec((tm, d), lambda i: (i, 0)),
        out_shape=jax.ShapeDtypeStruct((t, d), F32),
        scratch_shapes=[
            pltpu.VMEM((tm + HALO, d), BF16),
            pltpu.VMEM((tm + HALO, FFN_CHUNK), F32),
            pltpu.VMEM((tm, d), F32),
        ],
        compiler_params=_params("parallel"),
    )(x2d, x2d, w_up.astype(BF16), conv_w, conv_b.reshape(1, -1), w_down.astype(BF16),
      g.reshape(1, d), b.reshape(1, d))


def _sortable_key(score):
    bits = lax.bitcast_convert_type(score, I32)
    key = jnp.where(bits < 0, bits ^ jnp.int32(0x7FFFFFFF), bits)
    return jnp.where(score == 0.0, 0, key)


def _dsa_body(q_ref, qi_ref, wi_ref, k_ref, ki_ref, v_ref, o_ref,
              key_ref, lhs_ref, wb_ref, qm_ref, cmp_ref, thr_ref, s_ref, p_ref, m_ref, acc_ref,
              *, topk):
    qn, tk = DSA_Q, DSA_TK
    groups = tk // LANES
    i = pl.program_id(1)
    nkt = (i * qn + qn + tk - 1) // tk
    lane = lax.broadcasted_iota(I32, (qn, LANES), 1)
    low = lane < 64
    row_pos = i * qn + lax.broadcasted_iota(I32, (qn, LANES), 0)

    for p in range(IDX_HEADS // 2):
        pair = qi_ref[:, p * LANES:(p + 1) * LANES]
        lhs_ref[2 * p * qn:(2 * p + 1) * qn, :] = jnp.where(low, pair, jnp.zeros_like(pair))
        lhs_ref[(2 * p + 1) * qn:(2 * p + 2) * qn, :] = jnp.where(low, jnp.zeros_like(pair), pair)
    for h in range(IDX_HEADS):
        wb_ref[h] = jnp.broadcast_to(wi_ref[:, h:h + 1], (qn, LANES))

    def index_tile(kt, carry):
        ki_t = ki_ref[pl.ds(pl.multiple_of(kt * tk, tk), tk), :]
        lg_all = _nt_dot(lhs_ref[...], ki_t)
        acc = [jnp.zeros((qn, LANES), F32) for _ in range(groups)]
        for h in range(IDX_HEADS):
            lg = lg_all[h * qn:(h + 1) * qn]
            wb = wb_ref[h]
            for g in range(groups):
                acc[g] = acc[g] + wb * jnp.maximum(lg[:, g * LANES:(g + 1) * LANES], 0.0)
        for g in range(groups):
            col = kt * tk + g * LANES + lane
            key_ref[kt, :, g * LANES:(g + 1) * LANES] = jnp.where(
                col <= row_pos, _sortable_key(acc[g]), INT_MIN)
        return carry

    lax.fori_loop(0, nkt, index_tile, 0)

    def count(pred):
        def body(kt, acc):
            ref_val = cmp_ref[...]
            for g in range(groups):
                keys = key_ref[kt, :, g * LANES:(g + 1) * LANES]
                acc = acc + jnp.where(pred(keys, ref_val, kt * tk + g * LANES + lane), 1.0, 0.0)
            return acc
        acc = lax.fori_loop(0, nkt, body, jnp.zeros((qn, LANES), F32))
        return jnp.sum(acc, axis=-1, keepdims=True)

    def bit_step(it, t_u):
        cand_u = t_u | jnp.left_shift(jnp.int32(1), 31 - it)
        cmp_ref[...] = jnp.broadcast_to(cand_u ^ INT_MIN, (qn, LANES))
        cnt = count(lambda keys, cand, pos: keys >= cand)
        return jnp.where(cnt >= topk, cand_u, t_u)

    t_u = lax.fori_loop(0, 32, bit_step, jnp.zeros((qn, 1), I32))
    thr1 = jnp.maximum(t_u ^ INT_MIN, INT_MIN + 1)
    thr_ref[...] = jnp.broadcast_to(thr1, (qn, LANES))
    cmp_ref[...] = thr_ref[...]
    cnt_gt = count(lambda keys, thr, pos: keys > thr)
    cnt_eq = count(lambda keys, thr, pos: keys == thr)
    need = topk - cnt_gt
    excess_ties = jnp.max(jnp.where(cnt_eq > need, 1.0, 0.0)) > 0.0

    @pl.when(excess_ties)
    def _():
        pos_bits = max(1, int(np.ceil(np.log2(key_ref.shape[0] * tk))))

        def pos_step(it, p_lim):
            cand1 = p_lim | jnp.left_shift(jnp.int32(1), pos_bits - 1 - it)
            cmp_ref[...] = jnp.broadcast_to(cand1, (qn, LANES))
            cnt = count(lambda keys, cand, pos: (keys == thr_ref[...]) & (pos < cand))
            return jnp.where(cnt < need, cand1, p_lim)

        p_lim = jnp.broadcast_to(
            lax.fori_loop(0, pos_bits, pos_step, jnp.zeros((qn, 1), I32)), (qn, LANES))

        def demote(kt, carry):
            thr = thr_ref[...]
            for g in range(groups):
                keys = key_ref[kt, :, g * LANES:(g + 1) * LANES]
                pos = kt * tk + g * LANES + lane
                key_ref[kt, :, g * LANES:(g + 1) * LANES] = jnp.where(
                    (keys == thr) & (pos > p_lim), thr - 1, keys)
            return carry

        lax.fori_loop(0, nkt, demote, 0)

    half = DSA_HEADS // 2
    eye = jnp.where(lax.broadcasted_iota(I32, (qn, LANES), 0) == lane, 1.0, 0.0).astype(BF16)
    for p in range(half):
        pair = q_ref[:, p * LANES:(p + 1) * LANES]
        zero = jnp.zeros_like(pair)
        qm_ref[p * qn:(p + 1) * qn, 0:LANES] = jnp.where(low, pair, zero)
        qm_ref[(half + p) * qn:(half + p + 1) * qn, 0:LANES] = jnp.where(low, zero, pair)
    for h in range(DSA_HEADS):
        qm_ref[h * qn:(h + 1) * qn, LANES:2 * LANES] = eye
    m_ref[...] = jnp.full(m_ref.shape, NEG, F32)
    acc_ref[...] = jnp.zeros(acc_ref.shape, F32)
    low_k = lax.broadcasted_iota(I32, (tk, LANES), 1) < 64

    def scores(kt_raw):
        kt = jnp.minimum(kt_raw, nkt - 1)
        thr_t = jnp.where(kt_raw < nkt, thr_ref[...], jnp.int32(2 ** 31 - 1))
        bias = jnp.concatenate(
            [jnp.where(key_ref[kt, :, g * LANES:(g + 1) * LANES] >= thr_t, 0.0, NEG)
             for g in range(groups)], axis=1)
        k_aug = jnp.concatenate(
            [k_ref[pl.ds(pl.multiple_of(kt * tk, tk), tk), :], bias.T.astype(BF16)], axis=1)
        return _nt_dot(qm_ref[...], k_aug)

    def attend_tile(kt_raw, cur):
        s_ref[cur] = scores(kt_raw)
        kt = jnp.minimum(kt_raw, nkt - 1)
        v_t = v_ref[pl.ds(pl.multiple_of(kt * tk, tk), tk), :]
        one = jnp.ones_like(v_t)
        for h in range(DSA_HEADS):
            rows = slice(h * qn, (h + 1) * qn)
            sg = [s_ref[cur, rows, g * LANES:(g + 1) * LANES] for g in range(groups)]
            m_new = jnp.maximum(m_ref[1 - cur, h],
                                jnp.max(functools.reduce(jnp.maximum, sg), axis=-1, keepdims=True))
            m_ref[cur, h] = m_new
            for g in range(groups):
                p_ref[rows, g * LANES:(g + 1) * LANES] = jnp.exp2(sg[g] - m_new).astype(BF16)
        pv = (jnp.dot(p_ref[0:half * qn, :], jnp.where(low_k, v_t, one), preferred_element_type=F32),
              jnp.dot(p_ref[half * qn:, :], jnp.where(low_k, one, v_t), preferred_element_type=F32))
        for h in range(DSA_HEADS):
            alpha = jnp.exp2(m_ref[1 - cur, h] - m_ref[cur, h])
            acc_ref[h] = alpha * acc_ref[h] + pv[h // half][(h % half) * qn:(h % half + 1) * qn]

    def attend_pair(t, carry):
        attend_tile(2 * t, 0)
        attend_tile(2 * t + 1, 1)
        return carry

    lax.fori_loop(0, (nkt + 1) // 2, attend_pair, 0)

    for p in range(half):
        a0 = acc_ref[p]
        a1 = acc_ref[half + p]
        o0 = a0 / pltpu.roll(a0, 64, 1)
        o1 = a1 / pltpu.roll(a1, 64, 1)
        o_ref[:, p * LANES:(p + 1) * LANES] = jnp.where(low, o0, o1).astype(o_ref.dtype)


def _dsa_attention(proj, wi, bsz, seq, topk):
    t = bsz * seq
    qn, tk = DSA_Q, DSA_TK
    nqb = seq // qn
    nkt_max = seq // tk
    qcols = DSA_HEADS * DSA_HEAD_DIM
    icols = IDX_HEADS * IDX_DIM
    kcol = (qcols + icols) // LANES
    return pl.pallas_call(
        functools.partial(_dsa_body, topk=topk),
        grid=(bsz, nqb),
        in_specs=[
            pl.BlockSpec((qn, qcols), lambda b, i: (b * nqb + i, 0)),
            pl.BlockSpec((qn, icols), lambda b, i: (b * nqb + i, qcols // icols)),
            pl.BlockSpec((qn, LANES), lambda b, i: (b * nqb + i, 0)),
            pl.BlockSpec((seq, LANES), lambda b, i: (b, kcol)),
            pl.BlockSpec((seq, LANES), lambda b, i: (b, kcol + 1)),
            pl.BlockSpec((seq, LANES), lambda b, i: (b, kcol + 2)),
        ],
        out_specs=pl.BlockSpec((qn, qcols), lambda b, i: (b * nqb + i, 0)),
        out_shape=jax.ShapeDtypeStruct((t, qcols), BF16),
        scratch_shapes=[
            pltpu.VMEM((nkt_max, qn, tk), I32),
            pltpu.VMEM((IDX_HEADS * qn, LANES), BF16),
            pltpu.VMEM((IDX_HEADS, qn, LANES), F32),
            pltpu.VMEM((DSA_HEADS * qn, 2 * LANES), BF16),
            pltpu.VMEM((qn, LANES), I32),
            pltpu.VMEM((qn, LANES), I32),
            pltpu.VMEM((2, DSA_HEADS * qn, tk), F32),
            pltpu.VMEM((DSA_HEADS * qn, tk), BF16),
            pltpu.VMEM((2, DSA_HEADS, qn, LANES), F32),
            pltpu.VMEM((DSA_HEADS, qn, LANES), F32),
        ],
        compiler_params=_params("parallel", "arbitrary"),
    )(proj, proj, wi, proj, proj, proj)


def _dsa_layer(x2d, bsz, seq, w_in, w_out, ln_g, ln_b):
    h, dh, hi, di = DSA_HEADS, DSA_HEAD_DIM, IDX_HEADS, IDX_DIM
    offs = np.cumsum([h * dh, dh, dh, hi * di, di, hi])
    wq, wk, wv, wqi, wki, wwi = (w_in[:, a:b] for a, b in zip([0] + offs[:-1].tolist(), offs.tolist()))
    zpad = jnp.zeros((D_MODEL, LANES), F32)
    w_main = jnp.concatenate([wq, wqi, wk, wk, wki, wki, wv, wv, zpad], axis=1).astype(BF16)
    w_wi = jnp.concatenate([wwi, jnp.zeros((D_MODEL, LANES - hi), F32)], axis=1).astype(BF16)
    cos, sin = _rope_tables(seq, dh)
    scale = dh ** -0.5 * LOG2E
    chunks = [(0, 512, "rope64", scale), (512, 512, "rope64", scale), (1024, 512, "rope64", 1.0),
              (1536, 256, "rope64", 1.0), (1792, 256, "plain", 1.0)]
    proj = _proj(x2d, w_main, cos, sin, seq, chunks, BF16)
    wi = _proj(x2d, w_wi, cos, sin, seq, [(0, LANES, "plain", hi ** -0.5 * di ** -0.5)], F32)
    o = _dsa_attention(proj, wi, bsz, seq, min(DSA_TOPK, seq // 4))
    return _out_ln(o, w_out.astype(BF16), x2d, ln_g, ln_b)


def _moba_body(q_ref, k_ref, v_ref, km_ref, o_ref, qa_ref, m_ref, acc_ref, *, n_sel):
    blk = MOBA_BLOCK
    i = pl.program_id(2)
    lane = lax.broadcasted_iota(I32, (blk, LANES), 1)
    low = lane < 64
    def own(h, mine, other):
        return jnp.where(low, mine, other) if h == 0 else jnp.where(low, other, mine)

    zero = jnp.zeros((blk, LANES), BF16)
    one = jnp.ones((blk, LANES), BF16)
    rows = lax.broadcasted_iota(I32, (blk, blk), 0)
    cols = lax.broadcasted_iota(I32, (blk, blk), 1)
    lane_f = lane.astype(F32)
    start_own = pl.multiple_of(i * blk, blk)
    heads = 2 * MOBA_PAIRS_PER_STEP

    def pair_lanes(hd):
        return slice((hd // 2) * LANES, (hd // 2 + 1) * LANES)

    for hd in range(heads):
        h = hd % 2
        pair = q_ref[:, pair_lanes(hd)]
        km = km_ref[0, :, pair_lanes(hd)].astype(BF16)
        qh = own(h, pair, zero)
        gate = jnp.where(lane < i, _nt_dot(qh, km), NEG)
        chosen = lane < 0
        for _ in range(n_sel):
            best = jnp.max(gate, axis=-1, keepdims=True)
            first = jnp.min(jnp.where(gate == best, lane_f, float(LANES)), axis=-1, keepdims=True)
            hit = lane_f == first
            chosen = jnp.logical_or(chosen, hit)
            gate = jnp.where(hit, -jnp.inf, gate)
        pen = jnp.where(jnp.logical_and(chosen, lane < i), 0.0, NEG)
        if h == 0:
            pen = pltpu.roll(pen, 64, 1)
        qa_ref[hd] = own(h, pair, pen.astype(BF16))

        s = jnp.where(cols <= rows, _nt_dot(qh, k_ref[pl.ds(start_own, blk), pair_lanes(hd)]), NEG)
        m0 = jnp.max(s, axis=-1, keepdims=True)
        p = jnp.exp2(s - m0).astype(BF16)
        m_ref[hd] = jnp.broadcast_to(m0, (blk, LANES))
        v_aug = own(h, v_ref[pl.ds(start_own, blk), pair_lanes(hd)], one)
        acc_ref[hd] = jnp.dot(p, v_aug, preferred_element_type=F32)

    def past_pair(t, carry):
        starts = [pl.multiple_of((2 * t + u) * blk, blk) for u in range(2)]
        for hd in range(heads):
            h = hd % 2
            s = []
            for u in range(2):
                j = 2 * t + u
                block_lane = jnp.where(lane == (j + 64 if h == 0 else j), 1.0, 0.0).astype(BF16)
                k_aug = own(h, k_ref[pl.ds(starts[u], blk), pair_lanes(hd)], block_lane)
                s.append(_nt_dot(qa_ref[hd], k_aug))
            m_old = m_ref[hd]
            m_new = jnp.maximum(m_old, jnp.max(jnp.maximum(s[0], s[1]), axis=-1, keepdims=True))
            alpha = jnp.exp2(m_old - m_new)
            m_ref[hd] = m_new
            pv = [jnp.dot(jnp.exp2(s[u] - m_new[:, 0:1]).astype(BF16),
                          own(h, v_ref[pl.ds(starts[u], blk), pair_lanes(hd)], one),
                          preferred_element_type=F32) for u in range(2)]
            acc_ref[hd] = alpha * acc_ref[hd] + (pv[0] + pv[1])
        return carry

    lax.fori_loop(0, (i + 1) // 2, past_pair, 0)
    for pp in range(MOBA_PAIRS_PER_STEP):
        a0 = acc_ref[2 * pp]
        a1 = acc_ref[2 * pp + 1]
        o_ref[:, pp * LANES:(pp + 1) * LANES] = jnp.where(
            low, a0 / pltpu.roll(a0, 64, 1), a1 / pltpu.roll(a1, 64, 1)).astype(o_ref.dtype)


def _moba_attention(proj, kmean, bsz, seq):
    t = bsz * seq
    blk = MOBA_BLOCK
    nqb = seq // blk
    width = MOBA_PAIRS_PER_STEP * LANES
    groups = MOBA_HEADS * MOBA_HEAD_DIM // width
    heads = 2 * MOBA_PAIRS_PER_STEP
    n_sel = min(MOBA_TOPK, nqb - 1)
    return pl.pallas_call(
        functools.partial(_moba_body, n_sel=n_sel),
        grid=(bsz, groups, nqb),
        in_specs=[
            pl.BlockSpec((blk, width), lambda b, p, i: (b * nqb + i, p)),
            pl.BlockSpec((seq, width), lambda b, p, i: (b, groups + p)),
            pl.BlockSpec((seq, width), lambda b, p, i: (b, 2 * groups + p)),
            pl.BlockSpec((1, LANES, width), lambda b, p, i: (b, 0, p)),
        ],
        out_specs=pl.BlockSpec((blk, width), lambda b, p, i: (b * nqb + i, p)),
        out_shape=jax.ShapeDtypeStruct((t, MOBA_HEADS * MOBA_HEAD_DIM), BF16),
        scratch_shapes=[
            pltpu.VMEM((heads, blk, LANES), BF16),
            pltpu.VMEM((heads, blk, LANES), F32),
            pltpu.VMEM((heads, blk, LANES), F32),
        ],
        compiler_params=_params("parallel", "parallel", "arbitrary"),
    )(proj, proj, proj, kmean)


def _moba_layer(x2d, bsz, seq, w_in, w_out, ln_g, ln_b):
    hd = MOBA_HEADS * MOBA_HEAD_DIM
    cos, sin = _rope_tables(seq, MOBA_HEAD_DIM)
    scale = MOBA_HEAD_DIM ** -0.5 * LOG2E
    chunks = [(c0, 512, "rope64", scale if c0 < hd else 1.0) for c0 in range(0, 2 * hd, 512)]
    chunks += [(c0, 512, "plain", 1.0) for c0 in range(2 * hd, 3 * hd, 512)]
    proj, kmean = _proj(x2d, w_in.astype(BF16), cos, sin, seq, chunks, BF16,
                        block_mean=(hd, 2 * hd, MOBA_BLOCK))
    nb = seq // MOBA_BLOCK
    kmean = jnp.pad(kmean.reshape(bsz, nb, hd), ((0, 0), (0, LANES - nb), (0, 0)))
    o = _moba_attention(proj, kmean, bsz, seq)
    return _out_ln(o, w_out.astype(BF16), x2d, ln_g, ln_b)


def _ret_body(q_ref, k_ref, v_ref, g_ref, lg_ref, gng_ref, gnb_ref, o_ref, state_ref):
    c = pl.program_id(2)
    cc = RET_CHUNK

    @pl.when(c == 0)
    def _():
        state_ref[...] = jnp.zeros(state_ref.shape, F32)

    lg = lg_ref[0]
    ri = lax.broadcasted_iota(I32, (cc, cc), 0)
    ci = lax.broadcasted_iota(I32, (cc, cc), 1)
    rel = (ri - ci).astype(F32)
    lg_row = jnp.concatenate([lg] * (cc // LANES), axis=1)
    decay = jnp.where(rel >= 0, jnp.exp(jnp.maximum(rel, 0.0) * lg_row), 0.0)
    pos = lax.broadcasted_iota(I32, (cc, LANES), 0).astype(F32)
    q_decay = jnp.exp((pos + 1.0) * lg)[:, 0:1]
    k_decay = jnp.exp((cc - 1.0 - pos) * lg)[:, 0:1]
    chunk_decay = jnp.exp(cc * lg)[:, 0:1]

    q = q_ref[...]
    k = k_ref[...]
    v = v_ref[...]
    state = state_ref[...]
    inner = jnp.dot((_nt_dot(q, k) * decay).astype(BF16), v, preferred_element_type=F32)
    cross = jnp.dot(q, state.astype(BF16), preferred_element_type=F32) * q_decay
    kd_t = (k.astype(F32) * k_decay).T.astype(BF16)
    state_ref[...] = state * chunk_decay + jnp.dot(kd_t, v, preferred_element_type=F32)

    o = inner + cross
    mu = jnp.mean(o, axis=-1, keepdims=True)
    d = o - mu
    var = jnp.mean(d * d, axis=-1, keepdims=True)
    on = d * lax.rsqrt(var + GN_EPS) * gng_ref[...] + gnb_ref[...]
    gate = g_ref[...]
    o_ref[...] = (gate * jax.nn.sigmoid(gate) * on).astype(o_ref.dtype)


def _retention(proj, gate, lg, gn_g, gn_b, bsz, seq):
    t = bsz * seq
    cc = RET_CHUNK
    nc = seq // cc
    h, dk, dv = RET_HEADS, RET_QK_DIM, RET_V_DIM
    return pl.pallas_call(
        _ret_body,
        grid=(bsz, h, nc),
        in_specs=[
            pl.BlockSpec((cc, dk), lambda b, hh, c: (b * nc + c, hh)),
            pl.BlockSpec((cc, dk), lambda b, hh, c: (b * nc + c, h + hh)),
            pl.BlockSpec((cc, dv), lambda b, hh, c: (b * nc + c, (2 * h * dk) // dv + hh)),
            pl.BlockSpec((cc, dv), lambda b, hh, c: (b * nc + c, hh)),
            pl.BlockSpec((1, 1, LANES), lambda b, hh, c: (hh, 0, 0)),
            pl.BlockSpec((1, dv), lambda b, hh, c: (0, hh)),
            pl.BlockSpec((1, dv), lambda b, hh, c: (0, hh)),
        ],
        out_specs=pl.BlockSpec((cc, dv), lambda b, hh, c: (b * nc + c, hh)),
        out_shape=jax.ShapeDtypeStruct((t, h * dv), BF16),
        scratch_shapes=[pltpu.VMEM((dk, dv), F32)],
        compiler_params=_params("parallel", "parallel", "arbitrary"),
    )(proj, proj, proj, gate, lg, gn_g.reshape(1, -1), gn_b.reshape(1, -1))


def _retention_layer(x2d, bsz, seq, w_in, gn_g, gn_b, w_out, ln_g, ln_b):
    h, dk, dv = RET_HEADS, RET_QK_DIM, RET_V_DIM
    cos, sin = _rope_tables(seq, dk)
    nqk = 2 * h * dk
    chunks = [(c0, 512, "rope256", 1.0 if c0 < h * dk else dk ** -0.5) for c0 in range(0, nqk, 512)]
    chunks += [(c0, 512, "plain", 1.0) for c0 in range(nqk, nqk + h * dv, 512)]
    proj = _proj(x2d, w_in[:, :nqk + h * dv].astype(BF16), cos, sin, seq, chunks, BF16)
    gchunks = [(c0, 512, "plain", 1.0) for c0 in range(0, h * dv, 512)]
    gate = _proj(x2d, w_in[:, nqk + h * dv:].astype(BF16), cos, sin, seq, gchunks, F32)
    log_gamma = jnp.log1p(-(2.0 ** (-5.0 - jnp.arange(h, dtype=F32))))
    lg = jnp.broadcast_to(log_gamma[:, None, None], (h, 1, LANES))
    y = _retention(proj, gate, lg, gn_g, gn_b, bsz, seq)
    return _out_ln(y, w_out.astype(BF16), x2d, ln_g, ln_b)


def kernel(x, l0_w_in, l0_w_out, l0_ln1_g, l0_ln1_b, l0_w_up, l0_conv_w, l0_conv_b, l0_w_down, l0_ln2_g, l0_ln2_b, l1_w_in, l1_w_out, l1_ln1_g, l1_ln1_b, l1_w_up, l1_conv_w, l1_conv_b, l1_w_down, l1_ln2_g, l1_ln2_b, l2_w_in, l2_gn_g, l2_gn_b, l2_w_out, l2_ln1_g, l2_ln1_b, l2_w_up, l2_conv_w, l2_conv_b, l2_w_down, l2_ln2_g, l2_ln2_b, l3_w_in, l3_w_out, l3_ln1_g, l3_ln1_b, l3_w_up, l3_conv_w, l3_conv_b, l3_w_down, l3_ln2_g, l3_ln2_b):
    bsz, seq, d = x.shape
    h = x.reshape(bsz * seq, d)
    h = _dsa_layer(h, bsz, seq, l0_w_in, l0_w_out, l0_ln1_g, l0_ln1_b)
    h = _ffn_ln(h, seq, l0_w_up, l0_conv_w, l0_conv_b, l0_w_down, l0_ln2_g, l0_ln2_b)
    h = _moba_layer(h, bsz, seq, l1_w_in, l1_w_out, l1_ln1_g, l1_ln1_b)
    h = _ffn_ln(h, seq, l1_w_up, l1_conv_w, l1_conv_b, l1_w_down, l1_ln2_g, l1_ln2_b)
    h = _retention_layer(h, bsz, seq, l2_w_in, l2_gn_g, l2_gn_b, l2_w_out, l2_ln1_g, l2_ln1_b)
    h = _ffn_ln(h, seq, l2_w_up, l2_conv_w, l2_conv_b, l2_w_down, l2_ln2_g, l2_ln2_b)
    h = _dsa_layer(h, bsz, seq, l3_w_in, l3_w_out, l3_ln1_g, l3_ln1_b)
    h = _ffn_ln(h, seq, l3_w_up, l3_conv_w, l3_conv_b, l3_w_down, l3_ln2_g, l3_ln2_b)
    return h.reshape(bsz, seq, d)
```

```python
import functools

import jax
import jax.numpy as jnp
import numpy as np
from jax import lax
from jax.experimental import pallas as pl
from jax.experimental.pallas import tpu as pltpu

F32 = jnp.float32
BF16 = jnp.bfloat16
I32 = jnp.int32

LANES = 128
VMEM_LIMIT_BYTES = 56 * 1024 * 1024

D_MODEL = 1024
DEPTH = 4
ROPE_THETA = 10000.0
ALPHA = (2 * DEPTH) ** 0.25
LN_EPS = 1e-5
GN_EPS = 1e-6
NEG = -1e30
INT_MIN = -(2 ** 31)
LOG2E = 1.4426950408889634

DSA_HEADS = 16
DSA_HEAD_DIM = 64
IDX_HEADS = 8
IDX_DIM = 64
DSA_TOPK = 256
DSA_Q = 128
DSA_TK = 512

MOBA_HEADS = 16
MOBA_HEAD_DIM = 64
MOBA_BLOCK = 256
MOBA_TOPK = 3
MOBA_PAIRS_PER_STEP = 2
MOBA_QBLOCKS_PER_STEP = 4

RET_HEADS = 4
RET_QK_DIM = 256
RET_V_DIM = 512
RET_CHUNK = 256

D_FF = 2816
FFN_CHUNK = 256
ROW_TILE = 512
FFN_ROW_TILE = 1024
HALO = 16


def _params(*sem):
    return pltpu.CompilerParams(dimension_semantics=sem, vmem_limit_bytes=VMEM_LIMIT_BYTES)


def _nt_dot(a, b):
    return lax.dot_general(a, b, (((1,), (1,)), ((), ())), preferred_element_type=F32)


def _rope_tables(seq, dim):
    half = dim // 2
    inv = 1.0 / (ROPE_THETA ** (jnp.arange(0, dim, 2, dtype=F32) / dim))
    ang = jnp.arange(seq, dtype=F32)[:, None] * inv[None, :]
    cos, sin = jnp.cos(ang), jnp.sin(ang)
    if half >= LANES:
        return cos, sin
    reps = LANES // dim
    cos_t = jnp.tile(jnp.concatenate([cos, cos], axis=1), (1, reps))
    sin_t = jnp.tile(jnp.concatenate([-sin, sin], axis=1), (1, reps))
    return cos_t, sin_t


def _layer_norm_rows(y, g, b):
    mu = jnp.mean(y, axis=-1, keepdims=True)
    d = y - mu
    var = jnp.mean(d * d, axis=-1, keepdims=True)
    return d * lax.rsqrt(var + LN_EPS) * g + b


def _proj_body(x_ref, w_ref, cos_ref, sin_ref, *out_refs, chunks, block_mean):
    o_ref = out_refs[0]
    xb = x_ref[...].astype(BF16)
    tm = xb.shape[0]
    cos = cos_ref[...]
    sin = sin_ref[...]
    lane = lax.broadcasted_iota(I32, (tm, LANES), 1)
    first_half = (lane % 64) < 32
    for c0, width, kind, scale in chunks:
        y = jnp.dot(xb, w_ref[:, c0:c0 + width], preferred_element_type=F32)
        if kind == "rope64":
            parts = []
            for g in range(width // LANES):
                yg = y[:, g * LANES:(g + 1) * LANES]
                rot = jnp.where(first_half, pltpu.roll(yg, LANES - 32, 1), pltpu.roll(yg, 32, 1))
                parts.append(yg * cos + rot * sin)
            y = jnp.concatenate(parts, axis=1) if len(parts) > 1 else parts[0]
        elif kind == "rope256":
            parts = []
            for g in range(width // 256):
                x1 = y[:, g * 256:g * 256 + LANES]
                x2 = y[:, g * 256 + LANES:(g + 1) * 256]
                parts += [x1 * cos - x2 * sin, x1 * sin + x2 * cos]
            y = jnp.concatenate(parts, axis=1)
        if scale != 1.0:
            y = y * scale
        o_ref[:, c0:c0 + width] = y.astype(o_ref.dtype)
        if block_mean is not None and block_mean[0] <= c0 < block_mean[1]:
            km_ref = out_refs[1]
            blk = block_mean[2]
            for r in range(tm // blk):
                km_ref[r, :, c0 - block_mean[0]:c0 - block_mean[0] + width] = jnp.mean(
                    y[r * blk:(r + 1) * blk], axis=0, keepdims=True)


def _proj(x2d, w, cos, sin, seq, chunks, out_dtype, block_mean=None):
    t, d = x2d.shape
    n = w.shape[1]
    tm = ROW_TILE
    tiles_per_seq = seq // tm
    out_shape = [jax.ShapeDtypeStruct((t, n), out_dtype)]
    out_specs = [pl.BlockSpec((tm, n), lambda i: (i, 0))]
    if block_mean is not None:
        lo, hi, blk = block_mean
        out_shape.append(jax.ShapeDtypeStruct((t // blk, 1, hi - lo), F32))
        out_specs.append(pl.BlockSpec((tm // blk, 1, hi - lo), lambda i: (i, 0, 0)))
    res = pl.pallas_call(
        functools.partial(_proj_body, chunks=tuple(chunks), block_mean=block_mean),
        grid=(t // tm,),
        in_specs=[
            pl.BlockSpec((tm, d), lambda i: (i, 0)),
            pl.BlockSpec((d, n), lambda i: (0, 0)),
            pl.BlockSpec((tm, LANES), lambda i: (i % tiles_per_seq, 0)),
            pl.BlockSpec((tm, LANES), lambda i: (i % tiles_per_seq, 0)),
        ],
        out_specs=out_specs,
        out_shape=out_shape,
        compiler_params=_params("parallel"),
    )(x2d, w, cos, sin)
    return res if block_mean is not None else res[0]


def _out_ln_body(a_ref, w_ref, x_ref, g_ref, b_ref, o_ref):
    f = jnp.dot(a_ref[...], w_ref[...], preferred_element_type=F32)
    y = ALPHA * x_ref[...] + f
    o_ref[...] = _layer_norm_rows(y, g_ref[...], b_ref[...])


def _out_ln(a, w, x2d, g, b):
    t, kin = a.shape
    d = x2d.shape[1]
    tm = ROW_TILE
    return pl.pallas_call(
        _out_ln_body,
        grid=(t // tm,),
        in_specs=[
            pl.BlockSpec((tm, kin), lambda i: (i, 0)),
            pl.BlockSpec((kin, d), lambda i: (0, 0)),
            pl.BlockSpec((tm, d), lambda i: (i, 0)),
            pl.BlockSpec((1, d), lambda i: (0, 0)),
            pl.BlockSpec((1, d), lambda i: (0, 0)),
        ],
        out_specs=pl.BlockSpec((tm, d), lambda i: (i, 0)),
        out_shape=jax.ShapeDtypeStruct((t, d), F32),
        compiler_params=_params("parallel"),
    )(a, w, x2d, g.reshape(1, d), b.reshape(1, d))


def _ffn_body(x_ref, halo_ref, wup_ref, cw_ref, cb_ref, wdn_ref, g_ref, b_ref, o_ref,
              xe_ref, h_ref, acc_ref, *, tiles_per_seq):
    i = pl.program_id(0)
    tm = x_ref.shape[0]
    seq_start = (i % tiles_per_seq) == 0
    halo = halo_ref[...]
    xe_ref[0:HALO, :] = jnp.where(seq_start, 0.0, halo).astype(BF16)
    xe_ref[HALO:, :] = x_ref[...].astype(BF16)
    xe = xe_ref[...]

    def conv(c0):
        h_ref[...] = jnp.dot(xe, wup_ref[:, c0:c0 + FFN_CHUNK], preferred_element_type=F32)
        w = cw_ref[:, c0:c0 + FFN_CHUNK]
        return (h_ref[HALO - 2:HALO - 2 + tm, :] * w[0:1, :]
                + h_ref[HALO - 1:HALO - 1 + tm, :] * w[1:2, :]
                + h_ref[HALO:HALO + tm, :] * w[2:3, :]
                + cb_ref[:, c0:c0 + FFN_CHUNK])

    for c in range(D_FF // FFN_CHUNK):
        c0 = c * FFN_CHUNK
        gate = conv(c0)
        up = conv(D_FF + c0)
        act = (gate * jax.nn.sigmoid(gate) * up).astype(BF16)
        part = jnp.dot(act, wdn_ref[c0:c0 + FFN_CHUNK, :], preferred_element_type=F32)
        if c == 0:
            acc_ref[...] = part
        else:
            acc_ref[...] += part
    y = ALPHA * x_ref[...] + acc_ref[...]
    o_ref[...] = _layer_norm_rows(y, g_ref[...], b_ref[...])


def _ffn_ln(x2d, seq, w_up, conv_w, conv_b, w_down, g, b):
    t, d = x2d.shape
    tm = FFN_ROW_TILE
    tiles_per_seq = seq // tm
    halo_blocks = tm // HALO
    resident = pl.Buffered(1)
    return pl.pallas_call(
        functools.partial(_ffn_body, tiles_per_seq=tiles_per_seq),
        grid=(t // tm,),
        in_specs=[
            pl.BlockSpec((tm, d), lambda i: (i, 0)),
            pl.BlockSpec((HALO, d), lambda i: (jnp.maximum(i * halo_blocks - 1, 0), 0)),
            pl.BlockSpec((d, 2 * D_FF), lambda i: (0, 0), pipeline_mode=resident),
            pl.BlockSpec((3, 2 * D_FF), lambda i: (0, 0)),
            pl.BlockSpec((1, 2 * D_FF), lambda i: (0, 0)),
            pl.BlockSpec((D_FF, d), lambda i: (0, 0), pipeline_mode=resident),
            pl.BlockSpec((1, d), lambda i: (0, 0)),
            pl.BlockSpec((1, d), lambda i: (0, 0)),
        ],
        out_specs=pl.BlockSpec((tm, d), lambda i: (i, 0)),
        out_shape=jax.ShapeDtypeStruct((t, d), F32),
        scratch_shapes=[
            pltpu.VMEM((tm + HALO, d), BF16),
            pltpu.VMEM((tm + HALO, FFN_CHUNK), F32),
            pltpu.VMEM((tm, d), F32),
        ],
        compiler_params=_params("parallel"),
    )(x2d, x2d, w_up.astype(BF16), conv_w, conv_b.reshape(1, -1), w_down.astype(BF16),
      g.reshape(1, d), b.reshape(1, d))


def _sortable_key(score):
    bits = lax.bitcast_convert_type(score, I32)
    key = jnp.where(bits < 0, bits ^ jnp.int32(0x7FFFFFFF), bits)
    return jnp.where(score == 0.0, 0, key)


def _dsa_body(q_ref, qi_ref, wi_ref, k_ref, ki_ref, v_ref, o_ref,
              key_ref, lhs_ref, wb_ref, qm_ref, cmp_ref, thr_ref, s_ref, p_ref, m_ref, acc_ref,
              *, topk):
    qn, tk = DSA_Q, DSA_TK
    groups = tk // LANES
    i = pl.program_id(1)
    nkt = (i * qn + qn + tk - 1) // tk
    lane = lax.broadcasted_iota(I32, (qn, LANES), 1)
    low = lane < 64
    row_pos = i * qn + lax.broadcasted_iota(I32, (qn, LANES), 0)

    for p in range(IDX_HEADS // 2):
        pair = qi_ref[:, p * LANES:(p + 1) * LANES]
        lhs_ref[2 * p * qn:(2 * p + 1) * qn, :] = jnp.where(low, pair, jnp.zeros_like(pair))
        lhs_ref[(2 * p + 1) * qn:(2 * p + 2) * qn, :] = jnp.where(low, jnp.zeros_like(pair), pair)
    for h in range(IDX_HEADS):
        wb_ref[h] = jnp.broadcast_to(wi_ref[:, h:h + 1], (qn, LANES))

    def index_tile(kt, carry):
        ki_t = ki_ref[pl.ds(pl.multiple_of(kt * tk, tk), tk), :]
        lg_all = _nt_dot(lhs_ref[...], ki_t)
        acc = [jnp.zeros((qn, LANES), F32) for _ in range(groups)]
        for h in range(IDX_HEADS):
            lg = lg_all[h * qn:(h + 1) * qn]
            wb = wb_ref[h]
            for g in range(groups):
                acc[g] = acc[g] + wb * jnp.maximum(lg[:, g * LANES:(g + 1) * LANES], 0.0)
        for g in range(groups):
            col = kt * tk + g * LANES + lane
            key_ref[kt, :, g * LANES:(g + 1) * LANES] = jnp.where(
                col <= row_pos, _sortable_key(acc[g]), INT_MIN)
        return carry

    lax.fori_loop(0, nkt, index_tile, 0)

    def count(pred):
        def body(kt, acc):
            ref_val = cmp_ref[...]
            for g in range(groups):
                keys = key_ref[kt, :, g * LANES:(g + 1) * LANES]
                acc = acc + jnp.where(pred(keys, ref_val, kt * tk + g * LANES + lane), 1.0, 0.0)
            return acc
        acc = lax.fori_loop(0, nkt, body, jnp.zeros((qn, LANES), F32))
        return jnp.sum(acc, axis=-1, keepdims=True)

    def bit_step(it, t_u):
        cand_u = t_u | jnp.left_shift(jnp.int32(1), 31 - it)
        cmp_ref[...] = jnp.broadcast_to(cand_u ^ INT_MIN, (qn, LANES))
        cnt = count(lambda keys, cand, pos: keys >= cand)
        return jnp.where(cnt >= topk, cand_u, t_u)

    t_u = lax.fori_loop(0, 32, bit_step, jnp.zeros((qn, 1), I32))
    thr1 = jnp.maximum(t_u ^ INT_MIN, INT_MIN + 1)
    thr_ref[...] = jnp.broadcast_to(thr1, (qn, LANES))
    cmp_ref[...] = thr_ref[...]
    cnt_gt = count(lambda keys, thr, pos: keys > thr)
    cnt_eq = count(lambda keys, thr, pos: keys == thr)
    need = topk - cnt_gt
    excess_ties = jnp.max(jnp.where(cnt_eq > need, 1.0, 0.0)) > 0.0

    @pl.when(excess_ties)
    def _():
        pos_bits = max(1, int(np.ceil(np.log2(key_ref.shape[0] * tk))))

        def pos_step(it, p_lim):
            cand1 = p_lim | jnp.left_shift(jnp.int32(1), pos_bits - 1 - it)
            cmp_ref[...] = jnp.broadcast_to(cand1, (qn, LANES))
            cnt = count(lambda keys, cand, pos: (keys == thr_ref[...]) & (pos < cand))
            return jnp.where(cnt < need, cand1, p_lim)

        p_lim = jnp.broadcast_to(
            lax.fori_loop(0, pos_bits, pos_step, jnp.zeros((qn, 1), I32)), (qn, LANES))

        def demote(kt, carry):
            thr = thr_ref[...]
            for g in range(groups):
                keys = key_ref[kt, :, g * LANES:(g + 1) * LANES]
                pos = kt * tk + g * LANES + lane
                key_ref[kt, :, g * LANES:(g + 1) * LANES] = jnp.where(
                    (keys == thr) & (pos > p_lim), thr - 1, keys)
            return carry

        lax.fori_loop(0, nkt, demote, 0)

    half = DSA_HEADS // 2
    eye = jnp.where(lax.broadcasted_iota(I32, (qn, LANES), 0) == lane, 1.0, 0.0).astype(BF16)
    for p in range(half):
        pair = q_ref[:, p * LANES:(p + 1) * LANES]
        zero = jnp.zeros_like(pair)
        qm_ref[p * qn:(p + 1) * qn, 0:LANES] = jnp.where(low, pair, zero)
        qm_ref[(half + p) * qn:(half + p + 1) * qn, 0:LANES] = jnp.where(low, zero, pair)
    for h in range(DSA_HEADS):
        qm_ref[h * qn:(h + 1) * qn, LANES:2 * LANES] = eye
    m_ref[...] = jnp.full(m_ref.shape, NEG, F32)
    acc_ref[...] = jnp.zeros(acc_ref.shape, F32)
    low_k = lax.broadcasted_iota(I32, (tk, LANES), 1) < 64

    def scores(kt_raw):
        kt = jnp.minimum(kt_raw, nkt - 1)
        thr_t = jnp.where(kt_raw < nkt, thr_ref[...], jnp.int32(2 ** 31 - 1))
        bias = jnp.concatenate(
            [jnp.where(key_ref[kt, :, g * LANES:(g + 1) * LANES] >= thr_t, 0.0, NEG)
             for g in range(groups)], axis=1)
        k_aug = jnp.concatenate(
            [k_ref[pl.ds(pl.multiple_of(kt * tk, tk), tk), :], bias.T.astype(BF16)], axis=1)
        return _nt_dot(qm_ref[...], k_aug)

    def attend_tile(kt_raw, cur):
        s_ref[cur] = scores(kt_raw)
        kt = jnp.minimum(kt_raw, nkt - 1)
        v_t = v_ref[pl.ds(pl.multiple_of(kt * tk, tk), tk), :]
        one = jnp.ones_like(v_t)
        for h in range(DSA_HEADS):
            rows = slice(h * qn, (h + 1) * qn)
            sg = [s_ref[cur, rows, g * LANES:(g + 1) * LANES] for g in range(groups)]
            m_new = jnp.maximum(m_ref[1 - cur, h],
                                jnp.max(functools.reduce(jnp.maximum, sg), axis=-1, keepdims=True))
            m_ref[cur, h] = m_new
            for g in range(groups):
                p_ref[rows, g * LANES:(g + 1) * LANES] = jnp.exp2(sg[g] - m_new).astype(BF16)
        pv = (jnp.dot(p_ref[0:half * qn, :], jnp.where(low_k, v_t, one), preferred_element_type=F32),
              jnp.dot(p_ref[half * qn:, :], jnp.where(low_k, one, v_t), preferred_element_type=F32))
        for h in range(DSA_HEADS):
            alpha = jnp.exp2(m_ref[1 - cur, h] - m_ref[cur, h])
            acc_ref[h] = alpha * acc_ref[h] + pv[h // half][(h % half) * qn:(h % half + 1) * qn]

    def attend_pair(t, carry):
        attend_tile(2 * t, 0)
        attend_tile(2 * t + 1, 1)
        return carry

    lax.fori_loop(0, (nkt + 1) // 2, attend_pair, 0)

    for p in range(half):
        a0 = acc_ref[p]
        a1 = acc_ref[half + p]
        o0 = a0 / pltpu.roll(a0, 64, 1)
        o1 = a1 / pltpu.roll(a1, 64, 1)
        o_ref[:, p * LANES:(p + 1) * LANES] = jnp.where(low, o0, o1).astype(o_ref.dtype)


def _dsa_attention(proj, wi, bsz, seq, topk):
    t = bsz * seq
    qn, tk = DSA_Q, DSA_TK
    nqb = seq // qn
    nkt_max = seq // tk
    qcols = DSA_HEADS * DSA_HEAD_DIM
    icols = IDX_HEADS * IDX_DIM
    kcol = (qcols + icols) // LANES
    return pl.pallas_call(
        functools.partial(_dsa_body, topk=topk),
        grid=(bsz, nqb),
        in_specs=[
            pl.BlockSpec((qn, qcols), lambda b, i: (b * nqb + i, 0)),
            pl.BlockSpec((qn, icols), lambda b, i: (b * nqb + i, qcols // icols)),
            pl.BlockSpec((qn, LANES), lambda b, i: (b * nqb + i, 0)),
            pl.BlockSpec((seq, LANES), lambda b, i: (b, kcol)),
            pl.BlockSpec((seq, LANES), lambda b, i: (b, kcol + 1)),
            pl.BlockSpec((seq, LANES), lambda b, i: (b, kcol + 2)),
        ],
        out_specs=pl.BlockSpec((qn, qcols), lambda b, i: (b * nqb + i, 0)),
        out_shape=jax.ShapeDtypeStruct((t, qcols), BF16),
        scratch_shapes=[
            pltpu.VMEM((nkt_max, qn, tk), I32),
            pltpu.VMEM((IDX_HEADS * qn, LANES), BF16),
            pltpu.VMEM((IDX_HEADS, qn, LANES), F32),
            pltpu.VMEM((DSA_HEADS * qn, 2 * LANES), BF16),
            pltpu.VMEM((qn, LANES), I32),
            pltpu.VMEM((qn, LANES), I32),
            pltpu.VMEM((2, DSA_HEADS * qn, tk), F32),
            pltpu.VMEM((DSA_HEADS * qn, tk), BF16),
            pltpu.VMEM((2, DSA_HEADS, qn, LANES), F32),
            pltpu.VMEM((DSA_HEADS, qn, LANES), F32),
        ],
        compiler_params=_params("parallel", "arbitrary"),
    )(proj, proj, wi, proj, proj, proj)


def _dsa_layer(x2d, bsz, seq, w_in, w_out, ln_g, ln_b):
    h, dh, hi, di = DSA_HEADS, DSA_HEAD_DIM, IDX_HEADS, IDX_DIM
    offs = np.cumsum([h * dh, dh, dh, hi * di, di, hi])
    wq, wk, wv, wqi, wki, wwi = (w_in[:, a:b] for a, b in zip([0] + offs[:-1].tolist(), offs.tolist()))
    zpad = jnp.zeros((D_MODEL, LANES), F32)
    w_main = jnp.concatenate([wq, wqi, wk, wk, wki, wki, wv, wv, zpad], axis=1).astype(BF16)
    w_wi = jnp.concatenate([wwi, jnp.zeros((D_MODEL, LANES - hi), F32)], axis=1).astype(BF16)
    cos, sin = _rope_tables(seq, dh)
    scale = dh ** -0.5 * LOG2E
    chunks = [(0, 512, "rope64", scale), (512, 512, "rope64", scale), (1024, 512, "rope64", 1.0),
              (1536, 256, "rope64", 1.0), (1792, 256, "plain", 1.0)]
    proj = _proj(x2d, w_main, cos, sin, seq, chunks, BF16)
    wi = _proj(x2d, w_wi, cos, sin, seq, [(0, LANES, "plain", hi ** -0.5 * di ** -0.5)], F32)
    o = _dsa_attention(proj, wi, bsz, seq, min(DSA_TOPK, seq // 4))
    return _out_ln(o, w_out.astype(BF16), x2d, ln_g, ln_b)


def _moba_body(q_ref, k_ref, v_ref, km_ref, o_ref, qa_ref, m_ref, acc_ref, *, n_sel):
    blk = MOBA_BLOCK
    nq = MOBA_QBLOCKS_PER_STEP
    rows_n = nq * blk
    first_blk = pl.program_id(2) * nq
    lane = lax.broadcasted_iota(I32, (rows_n, LANES), 1)
    low = lane < 64
    own_blk = first_blk + lax.broadcasted_iota(I32, (rows_n, LANES), 0) // blk

    def own(h, mine, other):
        m = low if mine.shape[0] == rows_n else low_k
        return jnp.where(m, mine, other) if h == 0 else jnp.where(m, other, mine)

    zero = jnp.zeros((rows_n, LANES), BF16)
    one_k = jnp.ones((blk, LANES), BF16)
    r_loc = lax.broadcasted_iota(I32, (blk, blk), 0)
    c_loc = lax.broadcasted_iota(I32, (blk, blk), 1)
    lane_f = lane.astype(F32)
    lane_k = lax.broadcasted_iota(I32, (blk, LANES), 1)
    low_k = lane_k < 64
    heads = 2 * MOBA_PAIRS_PER_STEP

    def pair_lanes(hd):
        return slice((hd // 2) * LANES, (hd // 2 + 1) * LANES)

    def key_block(hd, j, with_block_id):
        h = hd % 2
        k_t = k_ref[pl.ds(pl.multiple_of(j * blk, blk), blk), pair_lanes(hd)]
        if not with_block_id:
            return k_t
        block_lane = jnp.where(lane_k == (j + 64 if h == 0 else j), 1.0, 0.0).astype(BF16)
        return own(h, k_t, block_lane)

    def value_block(hd, j):
        return own(hd % 2, v_ref[pl.ds(pl.multiple_of(j * blk, blk), blk), pair_lanes(hd)], one_k)

    for hd in range(heads):
        h = hd % 2
        pair = q_ref[:, pair_lanes(hd)]
        km = km_ref[0, :, pair_lanes(hd)].astype(BF16)
        qh = own(h, pair, zero)
        gate = jnp.where(lane < own_blk, _nt_dot(qh, km), NEG)
        chosen = lane < 0
        for _ in range(n_sel):
            best = jnp.max(gate, axis=-1, keepdims=True)
            first = jnp.min(jnp.where(gate == best, lane_f, float(LANES)), axis=-1, keepdims=True)
            hit = lane_f == first
            chosen = jnp.logical_or(chosen, hit)
            gate = jnp.where(hit, -jnp.inf, gate)
        pen = jnp.where(jnp.logical_and(chosen, lane < own_blk), 0.0, NEG)
        if h == 0:
            pen = pltpu.roll(pen, 64, 1)
        qa = own(h, pair, pen.astype(BF16))
        qa_ref[hd] = qa

        for a in range(nq):
            rs = slice(a * blk, (a + 1) * blk)
            s = [_nt_dot(qa[rs], key_block(hd, first_blk + bb, True)) for bb in range(a)]
            s.append(jnp.where(c_loc <= r_loc, _nt_dot(qh[rs], key_block(hd, first_blk + a, False)), NEG))
            m0 = jnp.max(functools.reduce(jnp.maximum, s), axis=-1, keepdims=True)
            m_ref[hd, rs, :] = jnp.broadcast_to(m0, (blk, LANES))
            acc_ref[hd, rs, :] = functools.reduce(jnp.add, [
                jnp.dot(jnp.exp2(s[bb] - m0).astype(BF16), value_block(hd, first_blk + bb),
                        preferred_element_type=F32) for bb in range(a + 1)])

    def past_pair(t, carry):
        for hd in range(heads):
            s = [_nt_dot(qa_ref[hd], key_block(hd, 2 * t + u, True)) for u in range(2)]
            m_old = m_ref[hd]
            m_new = jnp.maximum(m_old, jnp.max(jnp.maximum(s[0], s[1]), axis=-1, keepdims=True))
            alpha = jnp.exp2(m_old - m_new)
            m_ref[hd] = m_new
            pv = [jnp.dot(jnp.exp2(s[u] - m_new[:, 0:1]).astype(BF16), value_block(hd, 2 * t + u),
                          preferred_element_type=F32) for u in range(2)]
            acc_ref[hd] = alpha * acc_ref[hd] + (pv[0] + pv[1])
        return carry

    lax.fori_loop(0, first_blk // 2, past_pair, 0)
    for pp in range(MOBA_PAIRS_PER_STEP):
        a0 = acc_ref[2 * pp]
        a1 = acc_ref[2 * pp + 1]
        o_ref[:, pp * LANES:(pp + 1) * LANES] = jnp.where(
            low, a0 / pltpu.roll(a0, 64, 1), a1 / pltpu.roll(a1, 64, 1)).astype(o_ref.dtype)


def _moba_attention(proj, kmean, bsz, seq):
    t = bsz * seq
    rows_n = MOBA_QBLOCKS_PER_STEP * MOBA_BLOCK
    nsb = seq // rows_n
    width = MOBA_PAIRS_PER_STEP * LANES
    groups = MOBA_HEADS * MOBA_HEAD_DIM // width
    heads = 2 * MOBA_PAIRS_PER_STEP
    n_sel = min(MOBA_TOPK, seq // MOBA_BLOCK - 1)
    return pl.pallas_call(
        functools.partial(_moba_body, n_sel=n_sel),
        grid=(bsz, groups, nsb),
        in_specs=[
            pl.BlockSpec((rows_n, width), lambda b, p, i: (b * nsb + i, p)),
            pl.BlockSpec((seq, width), lambda b, p, i: (b, groups + p)),
            pl.BlockSpec((seq, width), lambda b, p, i: (b, 2 * groups + p)),
            pl.BlockSpec((1, LANES, width), lambda b, p, i: (b, 0, p)),
        ],
        out_specs=pl.BlockSpec((rows_n, width), lambda b, p, i: (b * nsb + i, p)),
        out_shape=jax.ShapeDtypeStruct((t, MOBA_HEADS * MOBA_HEAD_DIM), BF16),
        scratch_shapes=[
            pltpu.VMEM((heads, rows_n, LANES), BF16),
            pltpu.VMEM((heads, rows_n, LANES), F32),
            pltpu.VMEM((heads, rows_n, LANES), F32),
        ],
        compiler_params=_params("parallel", "parallel", "arbitrary"),
    )(proj, proj, proj, kmean)


def _moba_layer(x2d, bsz, seq, w_in, w_out, ln_g, ln_b):
    hd = MOBA_HEADS * MOBA_HEAD_DIM
    cos, sin = _rope_tables(seq, MOBA_HEAD_DIM)
    scale = MOBA_HEAD_DIM ** -0.5 * LOG2E
    chunks = [(c0, 512, "rope64", scale if c0 < hd else 1.0) for c0 in range(0, 2 * hd, 512)]
    chunks += [(c0, 512, "plain", 1.0) for c0 in range(2 * hd, 3 * hd, 512)]
    proj, kmean = _proj(x2d, w_in.astype(BF16), cos, sin, seq, chunks, BF16,
                        block_mean=(hd, 2 * hd, MOBA_BLOCK))
    nb = seq // MOBA_BLOCK
    kmean = jnp.pad(kmean.reshape(bsz, nb, hd), ((0, 0), (0, LANES - nb), (0, 0)))
    o = _moba_attention(proj, kmean, bsz, seq)
    return _out_ln(o, w_out.astype(BF16), x2d, ln_g, ln_b)


def _ret_body(q_ref, k_ref, v_ref, g_ref, lg_ref, gng_ref, gnb_ref, o_ref, state_ref):
    c = pl.program_id(2)
    cc = RET_CHUNK

    @pl.when(c == 0)
    def _():
        state_ref[...] = jnp.zeros(state_ref.shape, F32)

    lg = lg_ref[0]
    ri = lax.broadcasted_iota(I32, (cc, cc), 0)
    ci = lax.broadcasted_iota(I32, (cc, cc), 1)
    rel = (ri - ci).astype(F32)
    lg_row = jnp.concatenate([lg] * (cc // LANES), axis=1)
    decay = jnp.where(rel >= 0, jnp.exp(jnp.maximum(rel, 0.0) * lg_row), 0.0)
    pos = lax.broadcasted_iota(I32, (cc, LANES), 0).astype(F32)
    q_decay = jnp.exp((pos + 1.0) * lg)[:, 0:1]
    k_decay = jnp.exp((cc - 1.0 - pos) * lg)[:, 0:1]
    chunk_decay = jnp.exp(cc * lg)[:, 0:1]

    q = q_ref[...]
    k = k_ref[...]
    v = v_ref[...]
    state = state_ref[...]
    inner = jnp.dot((_nt_dot(q, k) * decay).astype(BF16), v, preferred_element_type=F32)
    cross = jnp.dot(q, state.astype(BF16), preferred_element_type=F32) * q_decay
    kd_t = (k.astype(F32) * k_decay).T.astype(BF16)
    state_ref[...] = state * chunk_decay + jnp.dot(kd_t, v, preferred_element_type=F32)

    o = inner + cross
    mu = jnp.mean(o, axis=-1, keepdims=True)
    d = o - mu
    var = jnp.mean(d * d, axis=-1, keepdims=True)
    on = d * lax.rsqrt(var + GN_EPS) * gng_ref[...] + gnb_ref[...]
    gate = g_ref[...]
    o_ref[...] = (gate * jax.nn.sigmoid(gate) * on).astype(o_ref.dtype)


def _retention(proj, gate, lg, gn_g, gn_b, bsz, seq):
    t = bsz * seq
    cc = RET_CHUNK
    nc = seq // cc
    h, dk, dv = RET_HEADS, RET_QK_DIM, RET_V_DIM
    return pl.pallas_call(
        _ret_body,
        grid=(bsz, h, nc),
        in_specs=[
            pl.BlockSpec((cc, dk), lambda b, hh, c: (b * nc + c, hh)),
            pl.BlockSpec((cc, dk), lambda b, hh, c: (b * nc + c, h + hh)),
            pl.BlockSpec((cc, dv), lambda b, hh, c: (b * nc + c, (2 * h * dk) // dv + hh)),
            pl.BlockSpec((cc, dv), lambda b, hh, c: (b * nc + c, hh)),
            pl.BlockSpec((1, 1, LANES), lambda b, hh, c: (hh, 0, 0)),
            pl.BlockSpec((1, dv), lambda b, hh, c: (0, hh)),
            pl.BlockSpec((1, dv), lambda b, hh, c: (0, hh)),
        ],
        out_specs=pl.BlockSpec((cc, dv), lambda b, hh, c: (b * nc + c, hh)),
        out_shape=jax.ShapeDtypeStruct((t, h * dv), BF16),
        scratch_shapes=[pltpu.VMEM((dk, dv), F32)],
        compiler_params=_params("parallel", "parallel", "arbitrary"),
    )(proj, proj, proj, gate, lg, gn_g.reshape(1, -1), gn_b.reshape(1, -1))


def _retention_layer(x2d, bsz, seq, w_in, gn_g, gn_b, w_out, ln_g, ln_b):
    h, dk, dv = RET_HEADS, RET_QK_DIM, RET_V_DIM
    cos, sin = _rope_tables(seq, dk)
    nqk = 2 * h * dk
    chunks = [(c0, 512, "rope256", 1.0 if c0 < h * dk else dk ** -0.5) for c0 in range(0, nqk, 512)]
    chunks += [(c0, 512, "plain", 1.0) for c0 in range(nqk, nqk + h * dv, 512)]
    proj = _proj(x2d, w_in[:, :nqk + h * dv].astype(BF16), cos, sin, seq, chunks, BF16)
    gchunks = [(c0, 512, "plain", 1.0) for c0 in range(0, h * dv, 512)]
    gate = _proj(x2d, w_in[:, nqk + h * dv:].astype(BF16), cos, sin, seq, gchunks, F32)
    log_gamma = jnp.log1p(-(2.0 ** (-5.0 - jnp.arange(h, dtype=F32))))
    lg = jnp.broadcast_to(log_gamma[:, None, None], (h, 1, LANES))
    y = _retention(proj, gate, lg, gn_g, gn_b, bsz, seq)
    return _out_ln(y, w_out.astype(BF16), x2d, ln_g, ln_b)


def kernel(x, l0_w_in, l0_w_out, l0_ln1_g, l0_ln1_b, l0_w_up, l0_conv_w, l0_conv_b, l0_w_down, l0_ln2_g, l0_ln2_b, l1_w_in, l1_w_out, l1_ln1_g, l1_ln1_b, l1_w_up, l1_conv_w, l1_conv_b, l1_w_down, l1_ln2_g, l1_ln2_b, l2_w_in, l2_gn_g, l2_gn_b, l2_w_out, l2_ln1_g, l2_ln1_b, l2_w_up, l2_conv_w, l2_conv_b, l2_w_down, l2_ln2_g, l2_ln2_b, l3_w_in, l3_w_out, l3_ln1_g, l3_ln1_b, l3_w_up, l3_conv_w, l3_conv_b, l3_w_down, l3_ln2_g, l3_ln2_b):
    bsz, seq, d = x.shape
    h = x.reshape(bsz * seq, d)
    h = _dsa_layer(h, bsz, seq, l0_w_in, l0_w_out, l0_ln1_g, l0_ln1_b)
    h = _ffn_ln(h, seq, l0_w_up, l0_conv_w, l0_conv_b, l0_w_down, l0_ln2_g, l0_ln2_b)
    h = _moba_layer(h, bsz, seq, l1_w_in, l1_w_out, l1_ln1_g, l1_ln1_b)
    h = _ffn_ln(h, seq, l1_w_up, l1_conv_w, l1_conv_b, l1_w_down, l1_ln2_g, l1_ln2_b)
    h = _retention_layer(h, bsz, seq, l2_w_in, l2_gn_g, l2_gn_b, l2_w_out, l2_ln1_g, l2_ln1_b)
    h = _ffn_ln(h, seq, l2_w_up, l2_conv_w, l2_conv_b, l2_w_down, l2_ln2_g, l2_ln2_b)
    h = _dsa_layer(h, bsz, seq, l3_w_in, l3_w_out, l3_ln1_g, l3_ln1_b)
    h = _ffn_ln(h, seq, l3_w_up, l3_conv_w, l3_conv_b, l3_w_down, l3_ln2_g, l3_ln2_b)
    return h.reshape(bsz, seq, d)
```

```python
import functools

import jax
import jax.numpy as jnp
import numpy as np
from jax import lax
from jax.experimental import pallas as pl
from jax.experimental.pallas import tpu as pltpu

F32 = jnp.float32
BF16 = jnp.bfloat16
I32 = jnp.int32

LANES = 128
VMEM_LIMIT_BYTES = 56 * 1024 * 1024

D_MODEL = 1024
DEPTH = 4
ROPE_THETA = 10000.0
ALPHA = (2 * DEPTH) ** 0.25
LN_EPS = 1e-5
GN_EPS = 1e-6
NEG = -1e30
INT_MIN = -(2 ** 31)
LOG2E = 1.4426950408889634

DSA_HEADS = 16
DSA_HEAD_DIM = 64
IDX_HEADS = 8
IDX_DIM = 64
DSA_TOPK = 256
DSA_Q = 128
DSA_TK = 512

MOBA_HEADS = 16
MOBA_HEAD_DIM = 64
MOBA_BLOCK = 256
MOBA_TOPK = 3
MOBA_PAIRS_PER_STEP = 2
MOBA_QBLOCKS_PER_STEP = 4

RET_HEADS = 4
RET_QK_DIM = 256
RET_V_DIM = 512
RET_CHUNK = 256

D_FF = 2816
FFN_CHUNK = 256
ROW_TILE = 512
FFN_ROW_TILE = 1024
HALO = 16


def _params(*sem):
    return pltpu.CompilerParams(dimension_semantics=sem, vmem_limit_bytes=VMEM_LIMIT_BYTES)


def _nt_dot(a, b):
    return lax.dot_general(a, b, (((1,), (1,)), ((), ())), preferred_element_type=F32)


def _rope_tables(seq, dim):
    half = dim // 2
    inv = 1.0 / (ROPE_THETA ** (jnp.arange(0, dim, 2, dtype=F32) / dim))
    ang = jnp.arange(seq, dtype=F32)[:, None] * inv[None, :]
    cos, sin = jnp.cos(ang), jnp.sin(ang)
    if half >= LANES:
        return cos, sin
    reps = LANES // dim
    cos_t = jnp.tile(jnp.concatenate([cos, cos], axis=1), (1, reps))
    sin_t = jnp.tile(jnp.concatenate([-sin, sin], axis=1), (1, reps))
    return cos_t, sin_t


def _layer_norm_rows(y, g, b):
    mu = jnp.mean(y, axis=-1, keepdims=True)
    d = y - mu
    var = jnp.mean(d * d, axis=-1, keepdims=True)
    return d * lax.rsqrt(var + LN_EPS) * g + b


def _proj_body(x_ref, w_ref, cos_ref, sin_ref, *out_refs, chunks, block_mean, side):
    o_ref = out_refs[0]
    xb = x_ref[...].astype(BF16)
    tm = xb.shape[0]
    cos = cos_ref[...]
    sin = sin_ref[...]
    lane = lax.broadcasted_iota(I32, (tm, LANES), 1)
    first_half = (lane % 64) < 32
    for c0, width, kind, scale in chunks:
        y = jnp.dot(xb, w_ref[:, c0:c0 + width], preferred_element_type=F32)
        if kind == "rope64":
            parts = []
            for g in range(width // LANES):
                yg = y[:, g * LANES:(g + 1) * LANES]
                rot = jnp.where(first_half, pltpu.roll(yg, LANES - 32, 1), pltpu.roll(yg, 32, 1))
                parts.append(yg * cos + rot * sin)
            y = jnp.concatenate(parts, axis=1) if len(parts) > 1 else parts[0]
        elif kind == "rope256":
            parts = []
            for g in range(width // 256):
                x1 = y[:, g * 256:g * 256 + LANES]
                x2 = y[:, g * 256 + LANES:(g + 1) * 256]
                parts += [x1 * cos - x2 * sin, x1 * sin + x2 * cos]
            y = jnp.concatenate(parts, axis=1)
        if scale != 1.0:
            y = y * scale
        o_ref[:, c0:c0 + width] = y.astype(o_ref.dtype)
        if block_mean is not None and block_mean[0] <= c0 < block_mean[1]:
            km_ref = out_refs[1]
            blk = block_mean[2]
            for r in range(tm // blk):
                km_ref[r, :, c0 - block_mean[0]:c0 - block_mean[0] + width] = jnp.mean(
                    y[r * blk:(r + 1) * blk], axis=0, keepdims=True)
    if side is not None:
        c0, width, scale = side
        out_refs[-1][...] = jnp.dot(xb, w_ref[:, c0:c0 + width], preferred_element_type=F32) * scale


def _proj(x2d, w, cos, sin, seq, chunks, out_dtype, block_mean=None, side=None):
    t, d = x2d.shape
    n = w.shape[1]
    tm = ROW_TILE
    tiles_per_seq = seq // tm
    out_shape = [jax.ShapeDtypeStruct((t, n), out_dtype)]
    out_specs = [pl.BlockSpec((tm, n), lambda i: (i, 0))]
    if block_mean is not None:
        lo, hi, blk = block_mean
        out_shape.append(jax.ShapeDtypeStruct((t // blk, 1, hi - lo), F32))
        out_specs.append(pl.BlockSpec((tm // blk, 1, hi - lo), lambda i: (i, 0, 0)))
    if side is not None:
        out_shape.append(jax.ShapeDtypeStruct((t, side[1]), F32))
        out_specs.append(pl.BlockSpec((tm, side[1]), lambda i: (i, 0)))
    res = pl.pallas_call(
        functools.partial(_proj_body, chunks=tuple(chunks), block_mean=block_mean, side=side),
        grid=(t // tm,),
        in_specs=[
            pl.BlockSpec((tm, d), lambda i: (i, 0)),
            pl.BlockSpec((d, n), lambda i: (0, 0)),
            pl.BlockSpec((tm, LANES), lambda i: (i % tiles_per_seq, 0)),
            pl.BlockSpec((tm, LANES), lambda i: (i % tiles_per_seq, 0)),
        ],
        out_specs=out_specs,
        out_shape=out_shape,
        compiler_params=_params("parallel"),
    )(x2d, w, cos, sin)
    return res if len(res) > 1 else res[0]


def _out_ln_body(a_ref, w_ref, x_ref, g_ref, b_ref, o_ref):
    f = jnp.dot(a_ref[...], w_ref[...], preferred_element_type=F32)
    y = ALPHA * x_ref[...] + f
    o_ref[...] = _layer_norm_rows(y, g_ref[...], b_ref[...])


def _out_ln(a, w, x2d, g, b):
    t, kin = a.shape
    d = x2d.shape[1]
    tm = ROW_TILE
    return pl.pallas_call(
        _out_ln_body,
        grid=(t // tm,),
        in_specs=[
            pl.BlockSpec((tm, kin), lambda i: (i, 0)),
            pl.BlockSpec((kin, d), lambda i: (0, 0)),
            pl.BlockSpec((tm, d), lambda i: (i, 0)),
            pl.BlockSpec((1, d), lambda i: (0, 0)),
            pl.BlockSpec((1, d), lambda i: (0, 0)),
        ],
        out_specs=pl.BlockSpec((tm, d), lambda i: (i, 0)),
        out_shape=jax.ShapeDtypeStruct((t, d), F32),
        compiler_params=_params("parallel"),
    )(a, w, x2d, g.reshape(1, d), b.reshape(1, d))


def _ffn_body(x_ref, halo_ref, wup_ref, cw_ref, cb_ref, wdn_ref, g_ref, b_ref, o_ref,
              xe_ref, h_ref, acc_ref, *, tiles_per_seq):
    i = pl.program_id(0)
    tm = x_ref.shape[0]
    seq_start = (i % tiles_per_seq) == 0
    halo = halo_ref[...]
    xe_ref[0:HALO, :] = jnp.where(seq_start, 0.0, halo).astype(BF16)
    xe_ref[HALO:, :] = x_ref[...].astype(BF16)
    xe = xe_ref[...]

    def conv(c0):
        h_ref[...] = jnp.dot(xe, wup_ref[:, c0:c0 + FFN_CHUNK], preferred_element_type=F32)
        w = cw_ref[:, c0:c0 + FFN_CHUNK]
        return (h_ref[HALO - 2:HALO - 2 + tm, :] * w[0:1, :]
                + h_ref[HALO - 1:HALO - 1 + tm, :] * w[1:2, :]
                + h_ref[HALO:HALO + tm, :] * w[2:3, :]
                + cb_ref[:, c0:c0 + FFN_CHUNK])

    for c in range(D_FF // FFN_CHUNK):
        c0 = c * FFN_CHUNK
        gate = conv(c0)
        up = conv(D_FF + c0)
        act = (gate * jax.nn.sigmoid(gate) * up).astype(BF16)
        part = jnp.dot(act, wdn_ref[c0:c0 + FFN_CHUNK, :], preferred_element_type=F32)
        if c == 0:
            acc_ref[...] = part
        else:
            acc_ref[...] += part
    y = ALPHA * x_ref[...] + acc_ref[...]
    o_ref[...] = _layer_norm_rows(y, g_ref[...], b_ref[...])


def _ffn_ln(x2d, seq, w_up, conv_w, conv_b, w_down, g, b):
    t, d = x2d.shape
    tm = FFN_ROW_TILE
    tiles_per_seq = seq // tm
    halo_blocks = tm // HALO
    resident = pl.Buffered(1)
    return pl.pallas_call(
        functools.partial(_ffn_body, tiles_per_seq=tiles_per_seq),
        grid=(t // tm,),
        in_specs=[
            pl.BlockSpec((tm, d), lambda i: (i, 0)),
            pl.BlockSpec((HALO, d), lambda i: (jnp.maximum(i * halo_blocks - 1, 0), 0)),
            pl.BlockSpec((d, 2 * D_FF), lambda i: (0, 0), pipeline_mode=resident),
            pl.BlockSpec((3, 2 * D_FF), lambda i: (0, 0)),
            pl.BlockSpec((1, 2 * D_FF), lambda i: (0, 0)),
            pl.BlockSpec((D_FF, d), lambda i: (0, 0), pipeline_mode=resident),
            pl.BlockSpec((1, d), lambda i: (0, 0)),
            pl.BlockSpec((1, d), lambda i: (0, 0)),
        ],
        out_specs=pl.BlockSpec((tm, d), lambda i: (i, 0)),
        out_shape=jax.ShapeDtypeStruct((t, d), F32),
        scratch_shapes=[
            pltpu.VMEM((tm + HALO, d), BF16),
            pltpu.VMEM((tm + HALO, FFN_CHUNK), F32),
            pltpu.VMEM((tm, d), F32),
        ],
        compiler_params=_params("parallel"),
    )(x2d, x2d, w_up.astype(BF16), conv_w, conv_b.reshape(1, -1), w_down.astype(BF16),
      g.reshape(1, d), b.reshape(1, d))


def _sortable_key(score):
    bits = lax.bitcast_convert_type(score, I32)
    key = jnp.where(bits < 0, bits ^ jnp.int32(0x7FFFFFFF), bits)
    return jnp.where(score == 0.0, 0, key)


def _dsa_body(q_ref, qi_ref, wi_ref, k_ref, ki_ref, v_ref, o_ref,
              key_ref, sb_ref, lhs_ref, wb_ref, qm_ref, cmp_ref, cmpb_ref, thr_ref, cnt_ref,
              s_ref, p_ref, m_ref, acc_ref, *, topk):
    qn, tk = DSA_Q, DSA_TK
    groups = tk // LANES
    i = pl.program_id(1)
    nkt = (i * qn + qn + tk - 1) // tk
    lane = lax.broadcasted_iota(I32, (qn, LANES), 1)
    low = lane < 64
    row_pos = i * qn + lax.broadcasted_iota(I32, (qn, LANES), 0)

    for p in range(IDX_HEADS // 2):
        pair = qi_ref[:, p * LANES:(p + 1) * LANES]
        lhs_ref[2 * p * qn:(2 * p + 1) * qn, :] = jnp.where(low, pair, jnp.zeros_like(pair))
        lhs_ref[(2 * p + 1) * qn:(2 * p + 2) * qn, :] = jnp.where(low, jnp.zeros_like(pair), pair)
    for h in range(IDX_HEADS):
        wb_ref[h] = jnp.broadcast_to(wi_ref[:, h:h + 1], (qn, LANES))

    def index_tile(kt, carry):
        ki_t = ki_ref[pl.ds(pl.multiple_of(kt * tk, tk), tk), :]
        lg_all = _nt_dot(lhs_ref[...], ki_t)
        acc = [jnp.zeros((qn, LANES), F32) for _ in range(groups)]
        for h in range(IDX_HEADS):
            lg = lg_all[h * qn:(h + 1) * qn]
            wb = wb_ref[h]
            for g in range(groups):
                acc[g] = acc[g] + wb * jnp.maximum(lg[:, g * LANES:(g + 1) * LANES], 0.0)
        for g in range(groups):
            causal = kt * tk + g * LANES + lane <= row_pos
            score = jnp.where(acc[g] == 0.0, 0.0, acc[g])
            bits = lax.bitcast_convert_type(score, I32)
            key_ref[kt, :, g * LANES:(g + 1) * LANES] = jnp.where(
                causal, jnp.where(bits < 0, bits ^ jnp.int32(0x7FFFFFFF), bits), INT_MIN)
            top = lax.bitcast_convert_type(bits & jnp.int32(-65536), F32)
            sb_ref[kt, :, g * LANES:(g + 1) * LANES] = jnp.where(causal, top, -jnp.inf).astype(BF16)
        return carry

    lax.fori_loop(0, nkt, index_tile, 0)

    def count(pred):
        def body(kt, acc):
            ref_val = cmp_ref[...]
            for g in range(groups):
                keys = key_ref[kt, :, g * LANES:(g + 1) * LANES]
                acc = acc + jnp.where(pred(keys, ref_val, kt * tk + g * LANES + lane), 1.0, 0.0)
            return acc
        acc = lax.fori_loop(0, nkt, body, jnp.zeros((qn, LANES), F32))
        return jnp.sum(acc, axis=-1, keepdims=True)

    def bit_step(it, t_u):
        cand_u = t_u | jnp.left_shift(jnp.int32(1), 31 - it)
        cmp_ref[...] = jnp.broadcast_to(cand_u ^ INT_MIN, (qn, LANES))
        cnt = count(lambda keys, cand, pos: keys >= cand)
        return jnp.where(cnt >= topk, cand_u, t_u)

    def coarse_step(it, t16):
        cand16 = t16 | jnp.left_shift(jnp.int32(1), 15 - it)
        k16 = cand16 - 32768
        raw16 = jnp.where(k16 >= 0, k16, (k16 ^ 0x7FFF) & 0xFFFF)
        cand = lax.bitcast_convert_type(jnp.left_shift(raw16, 16), F32)
        cmpb_ref[...] = jnp.broadcast_to(cand, (qn, LANES)).astype(BF16)

        def body(kt, acc):
            c = cmpb_ref[...]
            for g in range(groups):
                acc = acc + jnp.where(sb_ref[kt, :, g * LANES:(g + 1) * LANES] >= c,
                                      jnp.ones_like(c), jnp.zeros_like(c))
            return acc
        acc = lax.fori_loop(0, nkt, body, jnp.zeros((qn, LANES), BF16))
        cnt = jnp.sum(acc.astype(F32), axis=-1, keepdims=True)
        return jnp.where(cnt >= topk, cand16, t16)

    def finish(t_u):
        thr1 = jnp.maximum(t_u ^ INT_MIN, INT_MIN + 1)
        thr_ref[...] = jnp.broadcast_to(thr1, (qn, LANES))
        cmp_ref[...] = thr_ref[...]
        cnt_gt = count(lambda keys, thr, pos: keys > thr)
        cnt_eq = count(lambda keys, thr, pos: keys == thr)
        wrong = jnp.logical_or(cnt_gt >= topk, jnp.logical_and(cnt_gt + cnt_eq < topk, t_u != 0))
        cnt_ref[0] = jnp.broadcast_to(cnt_gt, (qn, LANES))
        cnt_ref[1] = jnp.broadcast_to(cnt_eq, (qn, LANES))
        return jnp.max(jnp.where(wrong, 1.0, 0.0)) > 0.0

    t16 = lax.fori_loop(0, 16, coarse_step, jnp.zeros((qn, 1), I32))
    coarse_failed = finish(lax.fori_loop(16, 32, bit_step, jnp.left_shift(t16, 16)))

    @pl.when(coarse_failed)
    def _():
        finish(lax.fori_loop(0, 32, bit_step, jnp.zeros((qn, 1), I32)))

    cnt_eq = cnt_ref[1][:, 0:1]
    need = topk - cnt_ref[0][:, 0:1]
    excess_ties = jnp.max(jnp.where(cnt_eq > need, 1.0, 0.0)) > 0.0

    @pl.when(excess_ties)
    def _():
        pos_bits = max(1, int(np.ceil(np.log2(key_ref.shape[0] * tk))))

        def pos_step(it, p_lim):
            cand1 = p_lim | jnp.left_shift(jnp.int32(1), pos_bits - 1 - it)
            cmp_ref[...] = jnp.broadcast_to(cand1, (qn, LANES))
            cnt = count(lambda keys, cand, pos: (keys == thr_ref[...]) & (pos < cand))
            return jnp.where(cnt < need, cand1, p_lim)

        p_lim = jnp.broadcast_to(
            lax.fori_loop(0, pos_bits, pos_step, jnp.zeros((qn, 1), I32)), (qn, LANES))

        def demote(kt, carry):
            thr = thr_ref[...]
            for g in range(groups):
                keys = key_ref[kt, :, g * LANES:(g + 1) * LANES]
                pos = kt * tk + g * LANES + lane
                key_ref[kt, :, g * LANES:(g + 1) * LANES] = jnp.where(
                    (keys == thr) & (pos > p_lim), thr - 1, keys)
            return carry

        lax.fori_loop(0, nkt, demote, 0)

    half = DSA_HEADS // 2
    eye = jnp.where(lax.broadcasted_iota(I32, (qn, LANES), 0) == lane, 1.0, 0.0).astype(BF16)
    for p in range(half):
        pair = q_ref[:, p * LANES:(p + 1) * LANES]
        zero = jnp.zeros_like(pair)
        qm_ref[p * qn:(p + 1) * qn, 0:LANES] = jnp.where(low, pair, zero)
        qm_ref[(half + p) * qn:(half + p + 1) * qn, 0:LANES] = jnp.where(low, zero, pair)
    for h in range(DSA_HEADS):
        qm_ref[h * qn:(h + 1) * qn, LANES:2 * LANES] = eye
    m_ref[...] = jnp.full(m_ref.shape, NEG, F32)
    acc_ref[...] = jnp.zeros(acc_ref.shape, F32)
    low_k = lax.broadcasted_iota(I32, (tk, LANES), 1) < 64

    def scores(kt_raw):
        kt = jnp.minimum(kt_raw, nkt - 1)
        thr_t = jnp.where(kt_raw < nkt, thr_ref[...], jnp.int32(2 ** 31 - 1))
        bias = jnp.concatenate(
            [jnp.where(key_ref[kt, :, g * LANES:(g + 1) * LANES] >= thr_t, 0.0, NEG)
             for g in range(groups)], axis=1)
        k_aug = jnp.concatenate(
            [k_ref[pl.ds(pl.multiple_of(kt * tk, tk), tk), :], bias.T.astype(BF16)], axis=1)
        return _nt_dot(qm_ref[...], k_aug)

    def attend_tile(kt_raw, cur):
        s_ref[cur] = scores(kt_raw)
        kt = jnp.minimum(kt_raw, nkt - 1)
        v_t = v_ref[pl.ds(pl.multiple_of(kt * tk, tk), tk), :]
        one = jnp.ones_like(v_t)
        for h in range(DSA_HEADS):
            rows = slice(h * qn, (h + 1) * qn)
            sg = [s_ref[cur, rows, g * LANES:(g + 1) * LANES] for g in range(groups)]
            m_new = jnp.maximum(m_ref[1 - cur, h],
                                jnp.max(functools.reduce(jnp.maximum, sg), axis=-1, keepdims=True))
            m_ref[cur, h] = m_new
            for g in range(groups):
                p_ref[rows, g * LANES:(g + 1) * LANES] = jnp.exp2(sg[g] - m_new).astype(BF16)
        pv = (jnp.dot(p_ref[0:half * qn, :], jnp.where(low_k, v_t, one), preferred_element_type=F32),
              jnp.dot(p_ref[half * qn:, :], jnp.where(low_k, one, v_t), preferred_element_type=F32))
        for h in range(DSA_HEADS):
            alpha = jnp.exp2(m_ref[1 - cur, h] - m_ref[cur, h])
            acc_ref[h] = alpha * acc_ref[h] + pv[h // half][(h % half) * qn:(h % half + 1) * qn]

    def attend_pair(t, carry):
        attend_tile(2 * t, 0)
        attend_tile(2 * t + 1, 1)
        return carry

    lax.fori_loop(0, (nkt + 1) // 2, attend_pair, 0)

    for p in range(half):
        a0 = acc_ref[p]
        a1 = acc_ref[half + p]
        o0 = a0 / pltpu.roll(a0, 64, 1)
        o1 = a1 / pltpu.roll(a1, 64, 1)
        o_ref[:, p * LANES:(p + 1) * LANES] = jnp.where(low, o0, o1).astype(o_ref.dtype)


def _dsa_attention(proj, wi, bsz, seq, topk):
    t = bsz * seq
    qn, tk = DSA_Q, DSA_TK
    nqb = seq // qn
    nkt_max = seq // tk
    qcols = DSA_HEADS * DSA_HEAD_DIM
    icols = IDX_HEADS * IDX_DIM
    kcol = (qcols + icols) // LANES
    return pl.pallas_call(
        functools.partial(_dsa_body, topk=topk),
        grid=(bsz, nqb),
        in_specs=[
            pl.BlockSpec((qn, qcols), lambda b, i: (b * nqb + i, 0)),
            pl.BlockSpec((qn, icols), lambda b, i: (b * nqb + i, qcols // icols)),
            pl.BlockSpec((qn, LANES), lambda b, i: (b * nqb + i, 0)),
            pl.BlockSpec((seq, LANES), lambda b, i: (b, kcol)),
            pl.BlockSpec((seq, LANES), lambda b, i: (b, kcol + 1)),
            pl.BlockSpec((seq, LANES), lambda b, i: (b, kcol + 2)),
        ],
        out_specs=pl.BlockSpec((qn, qcols), lambda b, i: (b * nqb + i, 0)),
        out_shape=jax.ShapeDtypeStruct((t, qcols), BF16),
        scratch_shapes=[
            pltpu.VMEM((nkt_max, qn, tk), I32),
            pltpu.VMEM((nkt_max, qn, tk), BF16),
            pltpu.VMEM((IDX_HEADS * qn, LANES), BF16),
            pltpu.VMEM((IDX_HEADS, qn, LANES), F32),
            pltpu.VMEM((DSA_HEADS * qn, 2 * LANES), BF16),
            pltpu.VMEM((qn, LANES), I32),
            pltpu.VMEM((qn, LANES), BF16),
            pltpu.VMEM((qn, LANES), I32),
            pltpu.VMEM((2, qn, LANES), F32),
            pltpu.VMEM((2, DSA_HEADS * qn, tk), F32),
            pltpu.VMEM((DSA_HEADS * qn, tk), BF16),
            pltpu.VMEM((2, DSA_HEADS, qn, LANES), F32),
            pltpu.VMEM((DSA_HEADS, qn, LANES), F32),
        ],
        compiler_params=_params("parallel", "arbitrary"),
    )(proj, proj, wi, proj, proj, proj)


def _dsa_layer(x2d, bsz, seq, w_in, w_out, ln_g, ln_b):
    h, dh, hi, di = DSA_HEADS, DSA_HEAD_DIM, IDX_HEADS, IDX_DIM
    offs = np.cumsum([h * dh, dh, dh, hi * di, di, hi])
    wq, wk, wv, wqi, wki, wwi = (w_in[:, a:b] for a, b in zip([0] + offs[:-1].tolist(), offs.tolist()))
    wwi_pad = jnp.concatenate([wwi, jnp.zeros((D_MODEL, LANES - hi), F32)], axis=1)
    w_main = jnp.concatenate([wq, wqi, wk, wk, wki, wki, wv, wv, wwi_pad], axis=1).astype(BF16)
    cos, sin = _rope_tables(seq, dh)
    scale = dh ** -0.5 * LOG2E
    chunks = [(0, 512, "rope64", scale), (512, 512, "rope64", scale), (1024, 512, "rope64", 1.0),
              (1536, 256, "rope64", 1.0), (1792, 256, "plain", 1.0)]
    proj, wi = _proj(x2d, w_main, cos, sin, seq, chunks, BF16,
                     side=(w_main.shape[1] - LANES, LANES, hi ** -0.5 * di ** -0.5))
    o = _dsa_attention(proj, wi, bsz, seq, min(DSA_TOPK, seq // 4))
    return _out_ln(o, w_out.astype(BF16), x2d, ln_g, ln_b)


def _moba_body(q_ref, k_ref, v_ref, km_ref, o_ref, qa_ref, m_ref, acc_ref, *, n_sel):
    blk = MOBA_BLOCK
    nq = MOBA_QBLOCKS_PER_STEP
    rows_n = nq * blk
    first_blk = pl.program_id(2) * nq
    lane = lax.broadcasted_iota(I32, (rows_n, LANES), 1)
    low = lane < 64
    own_blk = first_blk + lax.broadcasted_iota(I32, (rows_n, LANES), 0) // blk

    def own(h, mine, other):
        m = low if mine.shape[0] == rows_n else low_k
        return jnp.where(m, mine, other) if h == 0 else jnp.where(m, other, mine)

    zero = jnp.zeros((rows_n, LANES), BF16)
    one_k = jnp.ones((blk, LANES), BF16)
    r_loc = lax.broadcasted_iota(I32, (blk, blk), 0)
    c_loc = lax.broadcasted_iota(I32, (blk, blk), 1)
    lane_f = lane.astype(F32)
    lane_k = lax.broadcasted_iota(I32, (blk, LANES), 1)
    low_k = lane_k < 64
    heads = 2 * MOBA_PAIRS_PER_STEP

    def pair_lanes(hd):
        return slice((hd // 2) * LANES, (hd // 2 + 1) * LANES)

    def key_block(hd, j, with_block_id):
        h = hd % 2
        k_t = k_ref[pl.ds(pl.multiple_of(j * blk, blk), blk), pair_lanes(hd)]
        if not with_block_id:
            return k_t
        block_lane = jnp.where(lane_k == (j + 64 if h == 0 else j), 1.0, 0.0).astype(BF16)
        return own(h, k_t, block_lane)

    def value_block(hd, j):
        return own(hd % 2, v_ref[pl.ds(pl.multiple_of(j * blk, blk), blk), pair_lanes(hd)], one_k)

    for hd in range(heads):
        h = hd % 2
        pair = q_ref[:, pair_lanes(hd)]
        km = km_ref[0, :, pair_lanes(hd)].astype(BF16)
        qh = own(h, pair, zero)
        gate = jnp.where(lane < own_blk, _nt_dot(qh, km), NEG)
        chosen = lane < 0
        for _ in range(n_sel):
            best = jnp.max(gate, axis=-1, keepdims=True)
            first = jnp.min(jnp.where(gate == best, lane_f, float(LANES)), axis=-1, keepdims=True)
            hit = lane_f == first
            chosen = jnp.logical_or(chosen, hit)
            gate = jnp.where(hit, -jnp.inf, gate)
        pen = jnp.where(jnp.logical_and(chosen, lane < own_blk), 0.0, NEG)
        if h == 0:
            pen = pltpu.roll(pen, 64, 1)
        qa = own(h, pair, pen.astype(BF16))
        qa_ref[hd] = qa

        for a in range(nq):
            rs = slice(a * blk, (a + 1) * blk)
            s = [_nt_dot(qa[rs], key_block(hd, first_blk + bb, True)) for bb in range(a)]
            s.append(jnp.where(c_loc <= r_loc, _nt_dot(qh[rs], key_block(hd, first_blk + a, False)), NEG))
            m0 = jnp.max(functools.reduce(jnp.maximum, s), axis=-1, keepdims=True)
            m_ref[hd, rs, :] = jnp.broadcast_to(m0, (blk, LANES))
            acc_ref[hd, rs, :] = functools.reduce(jnp.add, [
                jnp.dot(jnp.exp2(s[bb] - m0).astype(BF16), value_block(hd, first_blk + bb),
                        preferred_element_type=F32) for bb in range(a + 1)])

    def past_pair(t, carry):
        for hd in range(heads):
            s = [_nt_dot(qa_ref[hd], key_block(hd, 2 * t + u, True)) for u in range(2)]
            m_old = m_ref[hd]
            m_new = jnp.maximum(m_old, jnp.max(jnp.maximum(s[0], s[1]), axis=-1, keepdims=True))
            alpha = jnp.exp2(m_old - m_new)
            m_ref[hd] = m_new
            pv = [jnp.dot(jnp.exp2(s[u] - m_new[:, 0:1]).astype(BF16), value_block(hd, 2 * t + u),
                          preferred_element_type=F32) for u in range(2)]
            acc_ref[hd] = alpha * acc_ref[hd] + (pv[0] + pv[1])
        return carry

    lax.fori_loop(0, first_blk // 2, past_pair, 0)
    for pp in range(MOBA_PAIRS_PER_STEP):
        a0 = acc_ref[2 * pp]
        a1 = acc_ref[2 * pp + 1]
        o_ref[:, pp * LANES:(pp + 1) * LANES] = jnp.where(
            low, a0 / pltpu.roll(a0, 64, 1), a1 / pltpu.roll(a1, 64, 1)).astype(o_ref.dtype)


def _moba_attention(proj, kmean, bsz, seq):
    t = bsz * seq
    rows_n = MOBA_QBLOCKS_PER_STEP * MOBA_BLOCK
    nsb = seq // rows_n
    width = MOBA_PAIRS_PER_STEP * LANES
    groups = MOBA_HEADS * MOBA_HEAD_DIM // width
    heads = 2 * MOBA_PAIRS_PER_STEP
    n_sel = min(MOBA_TOPK, seq // MOBA_BLOCK - 1)
    return pl.pallas_call(
        functools.partial(_moba_body, n_sel=n_sel),
        grid=(bsz, groups, nsb),
        in_specs=[
            pl.BlockSpec((rows_n, width), lambda b, p, i: (b * nsb + i, p)),
            pl.BlockSpec((seq, width), lambda b, p, i: (b, groups + p)),
            pl.BlockSpec((seq, width), lambda b, p, i: (b, 2 * groups + p)),
            pl.BlockSpec((1, LANES, width), lambda b, p, i: (b, 0, p)),
        ],
        out_specs=pl.BlockSpec((rows_n, width), lambda b, p, i: (b * nsb + i, p)),
        out_shape=jax.ShapeDtypeStruct((t, MOBA_HEADS * MOBA_HEAD_DIM), BF16),
        scratch_shapes=[
            pltpu.VMEM((heads, rows_n, LANES), BF16),
            pltpu.VMEM((heads, rows_n, LANES), F32),
            pltpu.VMEM((heads, rows_n, LANES), F32),
        ],
        compiler_params=_params("parallel", "parallel", "arbitrary"),
    )(proj, proj, proj, kmean)


def _moba_layer(x2d, bsz, seq, w_in, w_out, ln_g, ln_b):
    hd = MOBA_HEADS * MOBA_HEAD_DIM
    cos, sin = _rope_tables(seq, MOBA_HEAD_DIM)
    scale = MOBA_HEAD_DIM ** -0.5 * LOG2E
    chunks = [(c0, 512, "rope64", scale if c0 < hd else 1.0) for c0 in range(0, 2 * hd, 512)]
    chunks += [(c0, 512, "plain", 1.0) for c0 in range(2 * hd, 3 * hd, 512)]
    proj, kmean = _proj(x2d, w_in.astype(BF16), cos, sin, seq, chunks, BF16,
                        block_mean=(hd, 2 * hd, MOBA_BLOCK))
    nb = seq // MOBA_BLOCK
    kmean = jnp.pad(kmean.reshape(bsz, nb, hd), ((0, 0), (0, LANES - nb), (0, 0)))
    o = _moba_attention(proj, kmean, bsz, seq)
    return _out_ln(o, w_out.astype(BF16), x2d, ln_g, ln_b)


def _ret_body(q_ref, k_ref, v_ref, g_ref, lg_ref, gng_ref, gnb_ref, o_ref, state_ref):
    c = pl.program_id(2)
    cc = RET_CHUNK

    @pl.when(c == 0)
    def _():
        state_ref[...] = jnp.zeros(state_ref.shape, F32)

    lg = lg_ref[0]
    ri = lax.broadcasted_iota(I32, (cc, cc), 0)
    ci = lax.broadcasted_iota(I32, (cc, cc), 1)
    rel = (ri - ci).astype(F32)
    lg_row = jnp.concatenate([lg] * (cc // LANES), axis=1)
    decay = jnp.where(rel >= 0, jnp.exp(jnp.maximum(rel, 0.0) * lg_row), 0.0)
    pos = lax.broadcasted_iota(I32, (cc, LANES), 0).astype(F32)
    q_decay = jnp.exp((pos + 1.0) * lg)[:, 0:1]
    k_decay = jnp.exp((cc - 1.0 - pos) * lg)[:, 0:1]
    chunk_decay = jnp.exp(cc * lg)[:, 0:1]

    q = q_ref[...]
    k = k_ref[...]
    v = v_ref[...]
    state = state_ref[...]
    inner = jnp.dot((_nt_dot(q, k) * decay).astype(BF16), v, preferred_element_type=F32)
    cross = jnp.dot(q, state.astype(BF16), preferred_element_type=F32) * q_decay
    kd_t = (k.astype(F32) * k_decay).T.astype(BF16)
    state_ref[...] = state * chunk_decay + jnp.dot(kd_t, v, preferred_element_type=F32)

    o = inner + cross
    mu = jnp.mean(o, axis=-1, keepdims=True)
    d = o - mu
    var = jnp.mean(d * d, axis=-1, keepdims=True)
    on = d * lax.rsqrt(var + GN_EPS) * gng_ref[...] + gnb_ref[...]
    gate = g_ref[...]
    o_ref[...] = (gate * jax.nn.sigmoid(gate) * on).astype(o_ref.dtype)


def _retention(proj, gate, lg, gn_g, gn_b, bsz, seq):
    t = bsz * seq
    cc = RET_CHUNK
    nc = seq // cc
    h, dk, dv = RET_HEADS, RET_QK_DIM, RET_V_DIM
    return pl.pallas_call(
        _ret_body,
        grid=(bsz, h, nc),
        in_specs=[
            pl.BlockSpec((cc, dk), lambda b, hh, c: (b * nc + c, hh)),
            pl.BlockSpec((cc, dk), lambda b, hh, c: (b * nc + c, h + hh)),
            pl.BlockSpec((cc, dv), lambda b, hh, c: (b * nc + c, (2 * h * dk) // dv + hh)),
            pl.BlockSpec((cc, dv), lambda b, hh, c: (b * nc + c, hh)),
            pl.BlockSpec((1, 1, LANES), lambda b, hh, c: (hh, 0, 0)),
            pl.BlockSpec((1, dv), lambda b, hh, c: (0, hh)),
            pl.BlockSpec((1, dv), lambda b, hh, c: (0, hh)),
        ],
        out_specs=pl.BlockSpec((cc, dv), lambda b, hh, c: (b * nc + c, hh)),
        out_shape=jax.ShapeDtypeStruct((t, h * dv), BF16),
        scratch_shapes=[pltpu.VMEM((dk, dv), F32)],
        compiler_params=_params("parallel", "parallel", "arbitrary"),
    )(proj, proj, proj, gate, lg, gn_g.reshape(1, -1), gn_b.reshape(1, -1))


def _retention_layer(x2d, bsz, seq, w_in, gn_g, gn_b, w_out, ln_g, ln_b):
    h, dk, dv = RET_HEADS, RET_QK_DIM, RET_V_DIM
    cos, sin = _rope_tables(seq, dk)
    nqk = 2 * h * dk
    chunks = [(c0, 512, "rope256", 1.0 if c0 < h * dk else dk ** -0.5) for c0 in range(0, nqk, 512)]
    chunks += [(c0, 512, "plain", 1.0) for c0 in range(nqk, nqk + h * dv, 512)]
    proj = _proj(x2d, w_in[:, :nqk + h * dv].astype(BF16), cos, sin, seq, chunks, BF16)
    gchunks = [(c0, 512, "plain", 1.0) for c0 in range(0, h * dv, 512)]
    gate = _proj(x2d, w_in[:, nqk + h * dv:].astype(BF16), cos, sin, seq, gchunks, F32)
    log_gamma = jnp.log1p(-(2.0 ** (-5.0 - jnp.arange(h, dtype=F32))))
    lg = jnp.broadcast_to(log_gamma[:, None, None], (h, 1, LANES))
    y = _retention(proj, gate, lg, gn_g, gn_b, bsz, seq)
    return _out_ln(y, w_out.astype(BF16), x2d, ln_g, ln_b)


def kernel(x, l0_w_in, l0_w_out, l0_ln1_g, l0_ln1_b, l0_w_up, l0_conv_w, l0_conv_b, l0_w_down, l0_ln2_g, l0_ln2_b, l1_w_in, l1_w_out, l1_ln1_g, l1_ln1_b, l1_w_up, l1_conv_w, l1_conv_b, l1_w_down, l1_ln2_g, l1_ln2_b, l2_w_in, l2_gn_g, l2_gn_b, l2_w_out, l2_ln1_g, l2_ln1_b, l2_w_up, l2_conv_w, l2_conv_b, l2_w_down, l2_ln2_g, l2_ln2_b, l3_w_in, l3_w_out, l3_ln1_g, l3_ln1_b, l3_w_up, l3_conv_w, l3_conv_b, l3_w_down, l3_ln2_g, l3_ln2_b):
    bsz, seq, d = x.shape
    h = x.reshape(bsz * seq, d)
    h = _dsa_layer(h, bsz, seq, l0_w_in, l0_w_out, l0_ln1_g, l0_ln1_b)
    h = _ffn_ln(h, seq, l0_w_up, l0_conv_w, l0_conv_b, l0_w_down, l0_ln2_g, l0_ln2_b)
    h = _moba_layer(h, bsz, seq, l1_w_in, l1_w_out, l1_ln1_g, l1_ln1_b)
    h = _ffn_ln(h, seq, l1_w_up, l1_conv_w, l1_conv_b, l1_w_down, l1_ln2_g, l1_ln2_b)
    h = _retention_layer(h, bsz, seq, l2_w_in, l2_gn_g, l2_gn_b, l2_w_out, l2_ln1_g, l2_ln1_b)
    h = _ffn_ln(h, seq, l2_w_up, l2_conv_w, l2_conv_b, l2_w_down, l2_ln2_g, l2_ln2_b)
    h = _dsa_layer(h, bsz, seq, l3_w_in, l3_w_out, l3_ln1_g, l3_ln1_b)
    h = _ffn_ln(h, seq, l3_w_up, l3_conv_w, l3_conv_b, l3_w_down, l3_ln2_g, l3_ln2_b)
    return h.reshape(bsz, seq, d)
```

```python
import functools

import jax
import jax.numpy as jnp
import numpy as np
from jax import lax
from jax.experimental import pallas as pl
from jax.experimental.pallas import tpu as pltpu

F32 = jnp.float32
BF16 = jnp.bfloat16
I32 = jnp.int32

LANES = 128
VMEM_LIMIT_BYTES = 56 * 1024 * 1024

D_MODEL = 1024
DEPTH = 4
ROPE_THETA = 10000.0
ALPHA = (2 * DEPTH) ** 0.25
LN_EPS = 1e-5
GN_EPS = 1e-6
NEG = -1e30
INT_MIN = -(2 ** 31)
LOG2E = 1.4426950408889634

DSA_HEADS = 16
DSA_HEAD_DIM = 64
IDX_HEADS = 8
IDX_DIM = 64
DSA_TOPK = 256
DSA_Q = 128
DSA_TK = 512

MOBA_HEADS = 16
MOBA_HEAD_DIM = 64
MOBA_BLOCK = 256
MOBA_TOPK = 3
MOBA_PAIRS_PER_STEP = 2
MOBA_QBLOCKS_PER_STEP = 4

RET_HEADS = 4
RET_QK_DIM = 256
RET_V_DIM = 512
RET_CHUNK = 256

D_FF = 2816
FFN_CHUNK = 256
ROW_TILE = 512
FFN_ROW_TILE = 1024
HALO = 16


def _params(*sem):
    return pltpu.CompilerParams(dimension_semantics=sem, vmem_limit_bytes=VMEM_LIMIT_BYTES)


def _nt_dot(a, b):
    return lax.dot_general(a, b, (((1,), (1,)), ((), ())), preferred_element_type=F32)


def _rope_tables(seq, dim):
    half = dim // 2
    inv = 1.0 / (ROPE_THETA ** (jnp.arange(0, dim, 2, dtype=F32) / dim))
    ang = jnp.arange(seq, dtype=F32)[:, None] * inv[None, :]
    cos, sin = jnp.cos(ang), jnp.sin(ang)
    if half >= LANES:
        return cos, sin
    reps = LANES // dim
    cos_t = jnp.tile(jnp.concatenate([cos, cos], axis=1), (1, reps))
    sin_t = jnp.tile(jnp.concatenate([-sin, sin], axis=1), (1, reps))
    return cos_t, sin_t


def _layer_norm_rows(y, g, b):
    mu = jnp.mean(y, axis=-1, keepdims=True)
    d = y - mu
    var = jnp.mean(d * d, axis=-1, keepdims=True)
    return d * lax.rsqrt(var + LN_EPS) * g + b


def _proj_body(x_ref, w_ref, cos_ref, sin_ref, *out_refs, chunks, block_mean, side):
    o_ref = out_refs[0]
    xb = x_ref[...].astype(BF16)
    tm = xb.shape[0]
    cos = cos_ref[...]
    sin = sin_ref[...]
    lane = lax.broadcasted_iota(I32, (tm, LANES), 1)
    first_half = (lane % 64) < 32
    for c0, width, kind, scale in chunks:
        y = jnp.dot(xb, w_ref[:, c0:c0 + width], preferred_element_type=F32)
        if kind == "rope64":
            parts = []
            for g in range(width // LANES):
                yg = y[:, g * LANES:(g + 1) * LANES]
                rot = jnp.where(first_half, pltpu.roll(yg, LANES - 32, 1), pltpu.roll(yg, 32, 1))
                parts.append(yg * cos + rot * sin)
            y = jnp.concatenate(parts, axis=1) if len(parts) > 1 else parts[0]
        elif kind == "rope256":
            parts = []
            for g in range(width // 256):
                x1 = y[:, g * 256:g * 256 + LANES]
                x2 = y[:, g * 256 + LANES:(g + 1) * 256]
                parts += [x1 * cos - x2 * sin, x1 * sin + x2 * cos]
            y = jnp.concatenate(parts, axis=1)
        if scale != 1.0:
            y = y * scale
        o_ref[:, c0:c0 + width] = y.astype(o_ref.dtype)
        if block_mean is not None and block_mean[0] <= c0 < block_mean[1]:
            km_ref = out_refs[1]
            blk = block_mean[2]
            for r in range(tm // blk):
                km_ref[r, :, c0 - block_mean[0]:c0 - block_mean[0] + width] = jnp.mean(
                    y[r * blk:(r + 1) * blk], axis=0, keepdims=True)
    if side is not None:
        c0, width, scale = side
        out_refs[-1][...] = jnp.dot(xb, w_ref[:, c0:c0 + width], preferred_element_type=F32) * scale


def _proj(x2d, w, cos, sin, seq, chunks, out_dtype, block_mean=None, side=None):
    t, d = x2d.shape
    n = w.shape[1]
    tm = ROW_TILE
    tiles_per_seq = seq // tm
    out_shape = [jax.ShapeDtypeStruct((t, n), out_dtype)]
    out_specs = [pl.BlockSpec((tm, n), lambda i: (i, 0))]
    if block_mean is not None:
        lo, hi, blk = block_mean
        out_shape.append(jax.ShapeDtypeStruct((t // blk, 1, hi - lo), F32))
        out_specs.append(pl.BlockSpec((tm // blk, 1, hi - lo), lambda i: (i, 0, 0)))
    if side is not None:
        out_shape.append(jax.ShapeDtypeStruct((t, side[1]), F32))
        out_specs.append(pl.BlockSpec((tm, side[1]), lambda i: (i, 0)))
    res = pl.pallas_call(
        functools.partial(_proj_body, chunks=tuple(chunks), block_mean=block_mean, side=side),
        grid=(t // tm,),
        in_specs=[
            pl.BlockSpec((tm, d), lambda i: (i, 0)),
            pl.BlockSpec((d, n), lambda i: (0, 0)),
            pl.BlockSpec((tm, LANES), lambda i: (i % tiles_per_seq, 0)),
            pl.BlockSpec((tm, LANES), lambda i: (i % tiles_per_seq, 0)),
        ],
        out_specs=out_specs,
        out_shape=out_shape,
        compiler_params=_params("parallel"),
    )(x2d, w, cos, sin)
    return res if len(res) > 1 else res[0]


def _out_ln_body(a_ref, w_ref, x_ref, g_ref, b_ref, o_ref):
    f = jnp.dot(a_ref[...], w_ref[...], preferred_element_type=F32)
    y = ALPHA * x_ref[...] + f
    o_ref[...] = _layer_norm_rows(y, g_ref[...], b_ref[...])


def _out_ln(a, w, x2d, g, b):
    t, kin = a.shape
    d = x2d.shape[1]
    tm = ROW_TILE
    return pl.pallas_call(
        _out_ln_body,
        grid=(t // tm,),
        in_specs=[
            pl.BlockSpec((tm, kin), lambda i: (i, 0)),
            pl.BlockSpec((kin, d), lambda i: (0, 0)),
            pl.BlockSpec((tm, d), lambda i: (i, 0)),
            pl.BlockSpec((1, d), lambda i: (0, 0)),
            pl.BlockSpec((1, d), lambda i: (0, 0)),
        ],
        out_specs=pl.BlockSpec((tm, d), lambda i: (i, 0)),
        out_shape=jax.ShapeDtypeStruct((t, d), F32),
        compiler_params=_params("parallel"),
    )(a, w, x2d, g.reshape(1, d), b.reshape(1, d))


def _ffn_body(x_ref, halo_ref, wup_ref, cw_ref, cb_ref, wdn_ref, g_ref, b_ref, o_ref,
              xe_ref, h_ref, acc_ref, *, tiles_per_seq):
    i = pl.program_id(0)
    tm = x_ref.shape[0]
    seq_start = (i % tiles_per_seq) == 0
    halo = halo_ref[...]
    xe_ref[0:HALO, :] = jnp.where(seq_start, 0.0, halo).astype(BF16)
    xe_ref[HALO:, :] = x_ref[...].astype(BF16)
    xe = xe_ref[...]

    def conv(c0):
        h_ref[...] = jnp.dot(xe, wup_ref[:, c0:c0 + FFN_CHUNK], preferred_element_type=F32)
        w = cw_ref[:, c0:c0 + FFN_CHUNK]
        return (h_ref[HALO - 2:HALO - 2 + tm, :] * w[0:1, :]
                + h_ref[HALO - 1:HALO - 1 + tm, :] * w[1:2, :]
                + h_ref[HALO:HALO + tm, :] * w[2:3, :]
                + cb_ref[:, c0:c0 + FFN_CHUNK])

    for c in range(D_FF // FFN_CHUNK):
        c0 = c * FFN_CHUNK
        gate = conv(c0)
        up = conv(D_FF + c0)
        act = (gate * jax.nn.sigmoid(gate) * up).astype(BF16)
        part = jnp.dot(act, wdn_ref[c0:c0 + FFN_CHUNK, :], preferred_element_type=F32)
        if c == 0:
            acc_ref[...] = part
        else:
            acc_ref[...] += part
    y = ALPHA * x_ref[...] + acc_ref[...]
    o_ref[...] = _layer_norm_rows(y, g_ref[...], b_ref[...])


def _ffn_ln(x2d, seq, w_up, conv_w, conv_b, w_down, g, b):
    t, d = x2d.shape
    tm = FFN_ROW_TILE
    tiles_per_seq = seq // tm
    halo_blocks = tm // HALO
    resident = pl.Buffered(1)
    return pl.pallas_call(
        functools.partial(_ffn_body, tiles_per_seq=tiles_per_seq),
        grid=(t // tm,),
        in_specs=[
            pl.BlockSpec((tm, d), lambda i: (i, 0)),
            pl.BlockSpec((HALO, d), lambda i: (jnp.maximum(i * halo_blocks - 1, 0), 0)),
            pl.BlockSpec((d, 2 * D_FF), lambda i: (0, 0), pipeline_mode=resident),
            pl.BlockSpec((3, 2 * D_FF), lambda i: (0, 0)),
            pl.BlockSpec((1, 2 * D_FF), lambda i: (0, 0)),
            pl.BlockSpec((D_FF, d), lambda i: (0, 0), pipeline_mode=resident),
            pl.BlockSpec((1, d), lambda i: (0, 0)),
            pl.BlockSpec((1, d), lambda i: (0, 0)),
        ],
        out_specs=pl.BlockSpec((tm, d), lambda i: (i, 0)),
        out_shape=jax.ShapeDtypeStruct((t, d), F32),
        scratch_shapes=[
            pltpu.VMEM((tm + HALO, d), BF16),
            pltpu.VMEM((tm + HALO, FFN_CHUNK), F32),
            pltpu.VMEM((tm, d), F32),
        ],
        compiler_params=_params("parallel"),
    )(x2d, x2d, w_up.astype(BF16), conv_w, conv_b.reshape(1, -1), w_down.astype(BF16),
      g.reshape(1, d), b.reshape(1, d))


def _sortable_key(score):
    bits = lax.bitcast_convert_type(score, I32)
    key = jnp.where(bits < 0, bits ^ jnp.int32(0x7FFFFFFF), bits)
    return jnp.where(score == 0.0, 0, key)


def _dsa_body(q_ref, qi_ref, wi_ref, k_ref, ki_ref, v_ref, o_ref,
              key_ref, sb_ref, lhs_ref, wb_ref, qm_ref, cmp_ref, cmpb_ref, thr_ref, cnt_ref,
              s_ref, p_ref, m_ref, acc_ref, *, topk):
    qn, tk = DSA_Q, DSA_TK
    groups = tk // LANES
    i = pl.program_id(1)
    nkt = (i * qn + qn + tk - 1) // tk
    lane = lax.broadcasted_iota(I32, (qn, LANES), 1)
    low = lane < 64
    row_pos = i * qn + lax.broadcasted_iota(I32, (qn, LANES), 0)

    for p in range(IDX_HEADS // 2):
        pair = qi_ref[:, p * LANES:(p + 1) * LANES]
        lhs_ref[2 * p * qn:(2 * p + 1) * qn, :] = jnp.where(low, pair, jnp.zeros_like(pair))
        lhs_ref[(2 * p + 1) * qn:(2 * p + 2) * qn, :] = jnp.where(low, jnp.zeros_like(pair), pair)
    for h in range(IDX_HEADS):
        wb_ref[h] = jnp.broadcast_to(wi_ref[:, h:h + 1], (qn, LANES))

    def index_tile(kt, carry):
        ki_t = ki_ref[pl.ds(pl.multiple_of(kt * tk, tk), tk), :]
        lg_all = _nt_dot(lhs_ref[...], ki_t)
        acc = [jnp.zeros((qn, LANES), F32) for _ in range(groups)]
        for h in range(IDX_HEADS):
            lg = lg_all[h * qn:(h + 1) * qn]
            wb = wb_ref[h]
            for g in range(groups):
                acc[g] = acc[g] + wb * jnp.maximum(lg[:, g * LANES:(g + 1) * LANES], 0.0)
        for g in range(groups):
            causal = kt * tk + g * LANES + lane <= row_pos
            score = jnp.where(acc[g] == 0.0, 0.0, acc[g])
            bits = lax.bitcast_convert_type(score, I32)
            key_ref[kt, :, g * LANES:(g + 1) * LANES] = jnp.where(
                causal, jnp.where(bits < 0, bits ^ jnp.int32(0x7FFFFFFF), bits), INT_MIN)
            top = lax.bitcast_convert_type(bits & jnp.int32(-65536), F32)
            sb_ref[kt, :, g * LANES:(g + 1) * LANES] = jnp.where(causal, top, -jnp.inf).astype(BF16)
        return carry

    lax.fori_loop(0, nkt, index_tile, 0)

    def count(pred):
        def body(kt, acc):
            ref_val = cmp_ref[...]
            for g in range(groups):
                keys = key_ref[kt, :, g * LANES:(g + 1) * LANES]
                acc = acc + jnp.where(pred(keys, ref_val, kt * tk + g * LANES + lane), 1.0, 0.0)
            return acc
        acc = lax.fori_loop(0, nkt, body, jnp.zeros((qn, LANES), F32))
        return jnp.sum(acc, axis=-1, keepdims=True)

    def bit_step(it, t_u):
        cand_u = t_u | jnp.left_shift(jnp.int32(1), 31 - it)
        cmp_ref[...] = jnp.broadcast_to(cand_u ^ INT_MIN, (qn, LANES))
        cnt = count(lambda keys, cand, pos: keys >= cand)
        return jnp.where(cnt >= topk, cand_u, t_u)

    def coarse_step(it, t16):
        cand16 = t16 | jnp.left_shift(jnp.int32(1), 15 - it)
        k16 = cand16 - 32768
        raw16 = jnp.where(k16 >= 0, k16, (k16 ^ 0x7FFF) & 0xFFFF)
        raw16 = jnp.where(jnp.logical_and(raw16 > 0, raw16 < 0x80), 0x80, raw16)
        cand = lax.bitcast_convert_type(jnp.left_shift(raw16, 16), F32)
        cmpb_ref[...] = jnp.broadcast_to(cand, (qn, LANES)).astype(BF16)

        def body(kt, acc):
            c = cmpb_ref[...]
            for g in range(groups):
                acc = acc + jnp.where(sb_ref[kt, :, g * LANES:(g + 1) * LANES] >= c,
                                      jnp.ones_like(c), jnp.zeros_like(c))
            return acc
        acc = lax.fori_loop(0, nkt, body, jnp.zeros((qn, LANES), BF16))
        cnt = jnp.sum(acc.astype(F32), axis=-1, keepdims=True)
        return jnp.where(cnt >= topk, cand16, t16)

    def finish(t_u):
        thr1 = jnp.maximum(t_u ^ INT_MIN, INT_MIN + 1)
        thr_ref[...] = jnp.broadcast_to(thr1, (qn, LANES))
        cmp_ref[...] = thr_ref[...]
        cnt_gt = count(lambda keys, thr, pos: keys > thr)
        cnt_eq = count(lambda keys, thr, pos: keys == thr)
        wrong = jnp.logical_or(cnt_gt >= topk, jnp.logical_and(cnt_gt + cnt_eq < topk, t_u != 0))
        cnt_ref[0] = jnp.broadcast_to(cnt_gt, (qn, LANES))
        cnt_ref[1] = jnp.broadcast_to(cnt_eq, (qn, LANES))
        return jnp.max(jnp.where(wrong, 1.0, 0.0)) > 0.0

    t16 = lax.fori_loop(0, 16, coarse_step, jnp.zeros((qn, 1), I32))
    coarse_failed = finish(lax.fori_loop(16, 32, bit_step, jnp.left_shift(t16, 16)))

    @pl.when(coarse_failed)
    def _():
        finish(lax.fori_loop(0, 32, bit_step, jnp.zeros((qn, 1), I32)))

    cnt_eq = cnt_ref[1][:, 0:1]
    need = topk - cnt_ref[0][:, 0:1]
    excess_ties = jnp.max(jnp.where(cnt_eq > need, 1.0, 0.0)) > 0.0

    @pl.when(excess_ties)
    def _():
        pos_bits = max(1, int(np.ceil(np.log2(key_ref.shape[0] * tk))))

        def pos_step(it, p_lim):
            cand1 = p_lim | jnp.left_shift(jnp.int32(1), pos_bits - 1 - it)
            cmp_ref[...] = jnp.broadcast_to(cand1, (qn, LANES))
            cnt = count(lambda keys, cand, pos: (keys == thr_ref[...]) & (pos < cand))
            return jnp.where(cnt < need, cand1, p_lim)

        p_lim = jnp.broadcast_to(
            lax.fori_loop(0, pos_bits, pos_step, jnp.zeros((qn, 1), I32)), (qn, LANES))

        def demote(kt, carry):
            thr = thr_ref[...]
            for g in range(groups):
                keys = key_ref[kt, :, g * LANES:(g + 1) * LANES]
                pos = kt * tk + g * LANES + lane
                key_ref[kt, :, g * LANES:(g + 1) * LANES] = jnp.where(
                    (keys == thr) & (pos > p_lim), thr - 1, keys)
            return carry

        lax.fori_loop(0, nkt, demote, 0)

    half = DSA_HEADS // 2
    eye = jnp.where(lax.broadcasted_iota(I32, (qn, LANES), 0) == lane, 1.0, 0.0).astype(BF16)
    for p in range(half):
        pair = q_ref[:, p * LANES:(p + 1) * LANES]
        zero = jnp.zeros_like(pair)
        qm_ref[p * qn:(p + 1) * qn, 0:LANES] = jnp.where(low, pair, zero)
        qm_ref[(half + p) * qn:(half + p + 1) * qn, 0:LANES] = jnp.where(low, zero, pair)
    for h in range(DSA_HEADS):
        qm_ref[h * qn:(h + 1) * qn, LANES:2 * LANES] = eye
    m_ref[...] = jnp.full(m_ref.shape, NEG, F32)
    acc_ref[...] = jnp.zeros(acc_ref.shape, F32)
    low_k = lax.broadcasted_iota(I32, (tk, LANES), 1) < 64

    def scores(kt_raw):
        kt = jnp.minimum(kt_raw, nkt - 1)
        thr_t = jnp.where(kt_raw < nkt, thr_ref[...], jnp.int32(2 ** 31 - 1))
        bias = jnp.concatenate(
            [jnp.where(key_ref[kt, :, g * LANES:(g + 1) * LANES] >= thr_t, 0.0, NEG)
             for g in range(groups)], axis=1)
        k_aug = jnp.concatenate(
            [k_ref[pl.ds(pl.multiple_of(kt * tk, tk), tk), :], bias.T.astype(BF16)], axis=1)
        return _nt_dot(qm_ref[...], k_aug)

    def attend_tile(kt_raw, cur):
        s_ref[cur] = scores(kt_raw)
        kt = jnp.minimum(kt_raw, nkt - 1)
        v_t = v_ref[pl.ds(pl.multiple_of(kt * tk, tk), tk), :]
        one = jnp.ones_like(v_t)
        for h in range(DSA_HEADS):
            rows = slice(h * qn, (h + 1) * qn)
            sg = [s_ref[cur, rows, g * LANES:(g + 1) * LANES] for g in range(groups)]
            m_new = jnp.maximum(m_ref[1 - cur, h],
                                jnp.max(functools.reduce(jnp.maximum, sg), axis=-1, keepdims=True))
            m_ref[cur, h] = m_new
            for g in range(groups):
                p_ref[rows, g * LANES:(g + 1) * LANES] = jnp.exp2(sg[g] - m_new).astype(BF16)
        pv = (jnp.dot(p_ref[0:half * qn, :], jnp.where(low_k, v_t, one), preferred_element_type=F32),
              jnp.dot(p_ref[half * qn:, :], jnp.where(low_k, one, v_t), preferred_element_type=F32))
        for h in range(DSA_HEADS):
            alpha = jnp.exp2(m_ref[1 - cur, h] - m_ref[cur, h])
            acc_ref[h] = alpha * acc_ref[h] + pv[h // half][(h % half) * qn:(h % half + 1) * qn]

    def attend_pair(t, carry):
        attend_tile(2 * t, 0)
        attend_tile(2 * t + 1, 1)
        return carry

    lax.fori_loop(0, (nkt + 1) // 2, attend_pair, 0)

    for p in range(half):
        a0 = acc_ref[p]
        a1 = acc_ref[half + p]
        o0 = a0 / pltpu.roll(a0, 64, 1)
        o1 = a1 / pltpu.roll(a1, 64, 1)
        o_ref[:, p * LANES:(p + 1) * LANES] = jnp.where(low, o0, o1).astype(o_ref.dtype)


def _dsa_attention(proj, wi, bsz, seq, topk):
    t = bsz * seq
    qn, tk = DSA_Q, DSA_TK
    nqb = seq // qn
    nkt_max = seq // tk
    qcols = DSA_HEADS * DSA_HEAD_DIM
    icols = IDX_HEADS * IDX_DIM
    kcol = (qcols + icols) // LANES
    return pl.pallas_call(
        functools.partial(_dsa_body, topk=topk),
        grid=(bsz, nqb),
        in_specs=[
            pl.BlockSpec((qn, qcols), lambda b, i: (b * nqb + i, 0)),
            pl.BlockSpec((qn, icols), lambda b, i: (b * nqb + i, qcols // icols)),
            pl.BlockSpec((qn, LANES), lambda b, i: (b * nqb + i, 0)),
            pl.BlockSpec((seq, LANES), lambda b, i: (b, kcol)),
            pl.BlockSpec((seq, LANES), lambda b, i: (b, kcol + 1)),
            pl.BlockSpec((seq, LANES), lambda b, i: (b, kcol + 2)),
        ],
        out_specs=pl.BlockSpec((qn, qcols), lambda b, i: (b * nqb + i, 0)),
        out_shape=jax.ShapeDtypeStruct((t, qcols), BF16),
        scratch_shapes=[
            pltpu.VMEM((nkt_max, qn, tk), I32),
            pltpu.VMEM((nkt_max, qn, tk), BF16),
            pltpu.VMEM((IDX_HEADS * qn, LANES), BF16),
            pltpu.VMEM((IDX_HEADS, qn, LANES), F32),
            pltpu.VMEM((DSA_HEADS * qn, 2 * LANES), BF16),
            pltpu.VMEM((qn, LANES), I32),
            pltpu.VMEM((qn, LANES), BF16),
            pltpu.VMEM((qn, LANES), I32),
            pltpu.VMEM((2, qn, LANES), F32),
            pltpu.VMEM((2, DSA_HEADS * qn, tk), F32),
            pltpu.VMEM((DSA_HEADS * qn, tk), BF16),
            pltpu.VMEM((2, DSA_HEADS, qn, LANES), F32),
            pltpu.VMEM((DSA_HEADS, qn, LANES), F32),
        ],
        compiler_params=_params("parallel", "arbitrary"),
    )(proj, proj, wi, proj, proj, proj)


def _dsa_layer(x2d, bsz, seq, w_in, w_out, ln_g, ln_b):
    h, dh, hi, di = DSA_HEADS, DSA_HEAD_DIM, IDX_HEADS, IDX_DIM
    offs = np.cumsum([h * dh, dh, dh, hi * di, di, hi])
    wq, wk, wv, wqi, wki, wwi = (w_in[:, a:b] for a, b in zip([0] + offs[:-1].tolist(), offs.tolist()))
    wwi_pad = jnp.concatenate([wwi, jnp.zeros((D_MODEL, LANES - hi), F32)], axis=1)
    w_main = jnp.concatenate([wq, wqi, wk, wk, wki, wki, wv, wv, wwi_pad], axis=1).astype(BF16)
    cos, sin = _rope_tables(seq, dh)
    scale = dh ** -0.5 * LOG2E
    chunks = [(0, 512, "rope64", scale), (512, 512, "rope64", scale), (1024, 512, "rope64", 1.0),
              (1536, 256, "rope64", 1.0), (1792, 256, "plain", 1.0)]
    proj, wi = _proj(x2d, w_main, cos, sin, seq, chunks, BF16,
                     side=(w_main.shape[1] - LANES, LANES, hi ** -0.5 * di ** -0.5))
    o = _dsa_attention(proj, wi, bsz, seq, min(DSA_TOPK, seq // 4))
    return _out_ln(o, w_out.astype(BF16), x2d, ln_g, ln_b)


def _moba_body(q_ref, k_ref, v_ref, km_ref, o_ref, qa_ref, m_ref, acc_ref, *, n_sel):
    blk = MOBA_BLOCK
    nq = MOBA_QBLOCKS_PER_STEP
    rows_n = nq * blk
    first_blk = pl.program_id(2) * nq
    lane = lax.broadcasted_iota(I32, (rows_n, LANES), 1)
    low = lane < 64
    own_blk = first_blk + lax.broadcasted_iota(I32, (rows_n, LANES), 0) // blk

    def own(h, mine, other):
        m = low if mine.shape[0] == rows_n else low_k
        return jnp.where(m, mine, other) if h == 0 else jnp.where(m, other, mine)

    zero = jnp.zeros((rows_n, LANES), BF16)
    one_k = jnp.ones((blk, LANES), BF16)
    r_loc = lax.broadcasted_iota(I32, (blk, blk), 0)
    c_loc = lax.broadcasted_iota(I32, (blk, blk), 1)
    lane_f = lane.astype(F32)
    lane_k = lax.broadcasted_iota(I32, (blk, LANES), 1)
    low_k = lane_k < 64
    heads = 2 * MOBA_PAIRS_PER_STEP

    def pair_lanes(hd):
        return slice((hd // 2) * LANES, (hd // 2 + 1) * LANES)

    def key_block(hd, j, with_block_id):
        h = hd % 2
        k_t = k_ref[pl.ds(pl.multiple_of(j * blk, blk), blk), pair_lanes(hd)]
        if not with_block_id:
            return k_t
        block_lane = jnp.where(lane_k == (j + 64 if h == 0 else j), 1.0, 0.0).astype(BF16)
        return own(h, k_t, block_lane)

    def value_block(hd, j):
        return own(hd % 2, v_ref[pl.ds(pl.multiple_of(j * blk, blk), blk), pair_lanes(hd)], one_k)

    for hd in range(heads):
        h = hd % 2
        pair = q_ref[:, pair_lanes(hd)]
        km = km_ref[0, :, pair_lanes(hd)].astype(BF16)
        qh = own(h, pair, zero)
        gate = jnp.where(lane < own_blk, _nt_dot(qh, km), NEG)
        chosen = lane < 0
        for _ in range(n_sel):
            best = jnp.max(gate, axis=-1, keepdims=True)
            first = jnp.min(jnp.where(gate == best, lane_f, float(LANES)), axis=-1, keepdims=True)
            hit = lane_f == first
            chosen = jnp.logical_or(chosen, hit)
            gate = jnp.where(hit, -jnp.inf, gate)
        pen = jnp.where(jnp.logical_and(chosen, lane < own_blk), 0.0, NEG)
        if h == 0:
            pen = pltpu.roll(pen, 64, 1)
        qa = own(h, pair, pen.astype(BF16))
        qa_ref[hd] = qa

        for a in range(nq):
            rs = slice(a * blk, (a + 1) * blk)
            s = [_nt_dot(qa[rs], key_block(hd, first_blk + bb, True)) for bb in range(a)]
            s.append(jnp.where(c_loc <= r_loc, _nt_dot(qh[rs], key_block(hd, first_blk + a, False)), NEG))
            m0 = jnp.max(functools.reduce(jnp.maximum, s), axis=-1, keepdims=True)
            m_ref[hd, rs, :] = jnp.broadcast_to(m0, (blk, LANES))
            acc_ref[hd, rs, :] = functools.reduce(jnp.add, [
                jnp.dot(jnp.exp2(s[bb] - m0).astype(BF16), value_block(hd, first_blk + bb),
                        preferred_element_type=F32) for bb in range(a + 1)])

    def past_pair(t, carry):
        for hd in range(heads):
            s = [_nt_dot(qa_ref[hd], key_block(hd, 2 * t + u, True)) for u in range(2)]
            m_old = m_ref[hd]
            m_new = jnp.maximum(m_old, jnp.max(jnp.maximum(s[0], s[1]), axis=-1, keepdims=True))
            alpha = jnp.exp2(m_old - m_new)
            m_ref[hd] = m_new
            pv = [jnp.dot(jnp.exp2(s[u] - m_new[:, 0:1]).astype(BF16), value_block(hd, 2 * t + u),
                          preferred_element_type=F32) for u in range(2)]
            acc_ref[hd] = alpha * acc_ref[hd] + (pv[0] + pv[1])
        return carry

    lax.fori_loop(0, first_blk // 2, past_pair, 0)
    for pp in range(MOBA_PAIRS_PER_STEP):
        a0 = acc_ref[2 * pp]
        a1 = acc_ref[2 * pp + 1]
        o_ref[:, pp * LANES:(pp + 1) * LANES] = jnp.where(
            low, a0 / pltpu.roll(a0, 64, 1), a1 / pltpu.roll(a1, 64, 1)).astype(o_ref.dtype)


def _moba_attention(proj, kmean, bsz, seq):
    t = bsz * seq
    rows_n = MOBA_QBLOCKS_PER_STEP * MOBA_BLOCK
    nsb = seq // rows_n
    width = MOBA_PAIRS_PER_STEP * LANES
    groups = MOBA_HEADS * MOBA_HEAD_DIM // width
    heads = 2 * MOBA_PAIRS_PER_STEP
    n_sel = min(MOBA_TOPK, seq // MOBA_BLOCK - 1)
    return pl.pallas_call(
        functools.partial(_moba_body, n_sel=n_sel),
        grid=(bsz, groups, nsb),
        in_specs=[
            pl.BlockSpec((rows_n, width), lambda b, p, i: (b * nsb + i, p)),
            pl.BlockSpec((seq, width), lambda b, p, i: (b, groups + p)),
            pl.BlockSpec((seq, width), lambda b, p, i: (b, 2 * groups + p)),
            pl.BlockSpec((1, LANES, width), lambda b, p, i: (b, 0, p)),
        ],
        out_specs=pl.BlockSpec((rows_n, width), lambda b, p, i: (b * nsb + i, p)),
        out_shape=jax.ShapeDtypeStruct((t, MOBA_HEADS * MOBA_HEAD_DIM), BF16),
        scratch_shapes=[
            pltpu.VMEM((heads, rows_n, LANES), BF16),
            pltpu.VMEM((heads, rows_n, LANES), F32),
            pltpu.VMEM((heads, rows_n, LANES), F32),
        ],
        compiler_params=_params("parallel", "parallel", "arbitrary"),
    )(proj, proj, proj, kmean)


def _moba_layer(x2d, bsz, seq, w_in, w_out, ln_g, ln_b):
    hd = MOBA_HEADS * MOBA_HEAD_DIM
    cos, sin = _rope_tables(seq, MOBA_HEAD_DIM)
    scale = MOBA_HEAD_DIM ** -0.5 * LOG2E
    chunks = [(c0, 512, "rope64", scale if c0 < hd else 1.0) for c0 in range(0, 2 * hd, 512)]
    chunks += [(c0, 512, "plain", 1.0) for c0 in range(2 * hd, 3 * hd, 512)]
    proj, kmean = _proj(x2d, w_in.astype(BF16), cos, sin, seq, chunks, BF16,
                        block_mean=(hd, 2 * hd, MOBA_BLOCK))
    nb = seq // MOBA_BLOCK
    kmean = jnp.pad(kmean.reshape(bsz, nb, hd), ((0, 0), (0, LANES - nb), (0, 0)))
    o = _moba_attention(proj, kmean, bsz, seq)
    return _out_ln(o, w_out.astype(BF16), x2d, ln_g, ln_b)


def _ret_body(q_ref, k_ref, v_ref, g_ref, lg_ref, gng_ref, gnb_ref, o_ref, state_ref):
    c = pl.program_id(2)
    cc = RET_CHUNK

    @pl.when(c == 0)
    def _():
        state_ref[...] = jnp.zeros(state_ref.shape, F32)

    lg = lg_ref[0]
    ri = lax.broadcasted_iota(I32, (cc, cc), 0)
    ci = lax.broadcasted_iota(I32, (cc, cc), 1)
    rel = (ri - ci).astype(F32)
    lg_row = jnp.concatenate([lg] * (cc // LANES), axis=1)
    decay = jnp.where(rel >= 0, jnp.exp(jnp.maximum(rel, 0.0) * lg_row), 0.0)
    pos = lax.broadcasted_iota(I32, (cc, LANES), 0).astype(F32)
    q_decay = jnp.exp((pos + 1.0) * lg)[:, 0:1]
    k_decay = jnp.exp((cc - 1.0 - pos) * lg)[:, 0:1]
    chunk_decay = jnp.exp(cc * lg)[:, 0:1]

    q = q_ref[...]
    k = k_ref[...]
    v = v_ref[...]
    state = state_ref[...]
    inner = jnp.dot((_nt_dot(q, k) * decay).astype(BF16), v, preferred_element_type=F32)
    cross = jnp.dot(q, state.astype(BF16), preferred_element_type=F32) * q_decay
    kd_t = (k.astype(F32) * k_decay).T.astype(BF16)
    state_ref[...] = state * chunk_decay + jnp.dot(kd_t, v, preferred_element_type=F32)

    o = inner + cross
    mu = jnp.mean(o, axis=-1, keepdims=True)
    d = o - mu
    var = jnp.mean(d * d, axis=-1, keepdims=True)
    on = d * lax.rsqrt(var + GN_EPS) * gng_ref[...] + gnb_ref[...]
    gate = g_ref[...]
    o_ref[...] = (gate * jax.nn.sigmoid(gate) * on).astype(o_ref.dtype)


def _retention(proj, gate, lg, gn_g, gn_b, bsz, seq):
    t = bsz * seq
    cc = RET_CHUNK
    nc = seq // cc
    h, dk, dv = RET_HEADS, RET_QK_DIM, RET_V_DIM
    return pl.pallas_call(
        _ret_body,
        grid=(bsz, h, nc),
        in_specs=[
            pl.BlockSpec((cc, dk), lambda b, hh, c: (b * nc + c, hh)),
            pl.BlockSpec((cc, dk), lambda b, hh, c: (b * nc + c, h + hh)),
            pl.BlockSpec((cc, dv), lambda b, hh, c: (b * nc + c, (2 * h * dk) // dv + hh)),
            pl.BlockSpec((cc, dv), lambda b, hh, c: (b * nc + c, hh)),
            pl.BlockSpec((1, 1, LANES), lambda b, hh, c: (hh, 0, 0)),
            pl.BlockSpec((1, dv), lambda b, hh, c: (0, hh)),
            pl.BlockSpec((1, dv), lambda b, hh, c: (0, hh)),
        ],
        out_specs=pl.BlockSpec((cc, dv), lambda b, hh, c: (b * nc + c, hh)),
        out_shape=jax.ShapeDtypeStruct((t, h * dv), BF16),
        scratch_shapes=[pltpu.VMEM((dk, dv), F32)],
        compiler_params=_params("parallel", "parallel", "arbitrary"),
    )(proj, proj, proj, gate, lg, gn_g.reshape(1, -1), gn_b.reshape(1, -1))


def _retention_layer(x2d, bsz, seq, w_in, gn_g, gn_b, w_out, ln_g, ln_b):
    h, dk, dv = RET_HEADS, RET_QK_DIM, RET_V_DIM
    cos, sin = _rope_tables(seq, dk)
    nqk = 2 * h * dk
    chunks = [(c0, 512, "rope256", 1.0 if c0 < h * dk else dk ** -0.5) for c0 in range(0, nqk, 512)]
    chunks += [(c0, 512, "plain", 1.0) for c0 in range(nqk, nqk + h * dv, 512)]
    proj = _proj(x2d, w_in[:, :nqk + h * dv].astype(BF16), cos, sin, seq, chunks, BF16)
    gchunks = [(c0, 512, "plain", 1.0) for c0 in range(0, h * dv, 512)]
    gate = _proj(x2d, w_in[:, nqk + h * dv:].astype(BF16), cos, sin, seq, gchunks, F32)
    log_gamma = jnp.log1p(-(2.0 ** (-5.0 - jnp.arange(h, dtype=F32))))
    lg = jnp.broadcast_to(log_gamma[:, None, None], (h, 1, LANES))
    y = _retention(proj, gate, lg, gn_g, gn_b, bsz, seq)
    return _out_ln(y, w_out.astype(BF16), x2d, ln_g, ln_b)


def kernel(x, l0_w_in, l0_w_out, l0_ln1_g, l0_ln1_b, l0_w_up, l0_conv_w, l0_conv_b, l0_w_down, l0_ln2_g, l0_ln2_b, l1_w_in, l1_w_out, l1_ln1_g, l1_ln1_b, l1_w_up, l1_conv_w, l1_conv_b, l1_w_down, l1_ln2_g, l1_ln2_b, l2_w_in, l2_gn_g, l2_gn_b, l2_w_out, l2_ln1_g, l2_ln1_b, l2_w_up, l2_conv_w, l2_conv_b, l2_w_down, l2_ln2_g, l2_ln2_b, l3_w_in, l3_w_out, l3_ln1_g, l3_ln1_b, l3_w_up, l3_conv_w, l3_conv_b, l3_w_down, l3_ln2_g, l3_ln2_b):
    bsz, seq, d = x.shape
    h = x.reshape(bsz * seq, d)
    h = _dsa_layer(h, bsz, seq, l0_w_in, l0_w_out, l0_ln1_g, l0_ln1_b)
    h = _ffn_ln(h, seq, l0_w_up, l0_conv_w, l0_conv_b, l0_w_down, l0_ln2_g, l0_ln2_b)
    h = _moba_layer(h, bsz, seq, l1_w_in, l1_w_out, l1_ln1_g, l1_ln1_b)
    h = _ffn_ln(h, seq, l1_w_up, l1_conv_w, l1_conv_b, l1_w_down, l1_ln2_g, l1_ln2_b)
    h = _retention_layer(h, bsz, seq, l2_w_in, l2_gn_g, l2_gn_b, l2_w_out, l2_ln1_g, l2_ln1_b)
    h = _ffn_ln(h, seq, l2_w_up, l2_conv_w, l2_conv_b, l2_w_down, l2_ln2_g, l2_ln2_b)
    h = _dsa_layer(h, bsz, seq, l3_w_in, l3_w_out, l3_ln1_g, l3_ln1_b)
    h = _ffn_ln(h, seq, l3_w_up, l3_conv_w, l3_conv_b, l3_w_down, l3_ln2_g, l3_ln2_b)
    return h.reshape(bsz, seq, d)
```

```python
import functools

import jax
import jax.numpy as jnp
import numpy as np
from jax import lax
from jax.experimental import pallas as pl
from jax.experimental.pallas import tpu as pltpu

F32 = jnp.float32
BF16 = jnp.bfloat16
I32 = jnp.int32

LANES = 128
VMEM_LIMIT_BYTES = 56 * 1024 * 1024

D_MODEL = 1024
DEPTH = 4
ROPE_THETA = 10000.0
ALPHA = (2 * DEPTH) ** 0.25
LN_EPS = 1e-5
GN_EPS = 1e-6
NEG = -1e30
INT_MIN = -(2 ** 31)
LOG2E = 1.4426950408889634

DSA_HEADS = 16
DSA_HEAD_DIM = 64
IDX_HEADS = 8
IDX_DIM = 64
DSA_TOPK = 256
DSA_Q = 128
DSA_TK = 512

MOBA_HEADS = 16
MOBA_HEAD_DIM = 64
MOBA_BLOCK = 256
MOBA_TOPK = 3
MOBA_PAIRS_PER_STEP = 2
MOBA_QBLOCKS_PER_STEP = 4

RET_HEADS = 4
RET_QK_DIM = 256
RET_V_DIM = 512
RET_CHUNK = 256

D_FF = 2816
FFN_CHUNK = 256
ROW_TILE = 512
FFN_ROW_TILE = 512
HALO = 16


def _params(*sem):
    return pltpu.CompilerParams(dimension_semantics=sem, vmem_limit_bytes=VMEM_LIMIT_BYTES)


def _nt_dot(a, b):
    return lax.dot_general(a, b, (((1,), (1,)), ((), ())), preferred_element_type=F32)


def _rope_tables(seq, dim):
    half = dim // 2
    inv = 1.0 / (ROPE_THETA ** (jnp.arange(0, dim, 2, dtype=F32) / dim))
    ang = jnp.arange(seq, dtype=F32)[:, None] * inv[None, :]
    cos, sin = jnp.cos(ang), jnp.sin(ang)
    if half >= LANES:
        return cos, sin
    reps = LANES // dim
    cos_t = jnp.tile(jnp.concatenate([cos, cos], axis=1), (1, reps))
    sin_t = jnp.tile(jnp.concatenate([-sin, sin], axis=1), (1, reps))
    return cos_t, sin_t


def _layer_norm_rows(y, g, b):
    mu = jnp.mean(y, axis=-1, keepdims=True)
    d = y - mu
    var = jnp.mean(d * d, axis=-1, keepdims=True)
    return d * lax.rsqrt(var + LN_EPS) * g + b


def _proj_body(x_ref, w_ref, cos_ref, sin_ref, *out_refs, chunks, block_mean, side):
    o_ref = out_refs[0]
    xb = x_ref[...].astype(BF16)
    tm = xb.shape[0]
    cos = cos_ref[...]
    sin = sin_ref[...]
    lane = lax.broadcasted_iota(I32, (tm, LANES), 1)
    first_half = (lane % 64) < 32
    for c0, width, kind, scale in chunks:
        y = jnp.dot(xb, w_ref[:, c0:c0 + width], preferred_element_type=F32)
        if kind == "rope64":
            parts = []
            for g in range(width // LANES):
                yg = y[:, g * LANES:(g + 1) * LANES]
                rot = jnp.where(first_half, pltpu.roll(yg, LANES - 32, 1), pltpu.roll(yg, 32, 1))
                parts.append(yg * cos + rot * sin)
            y = jnp.concatenate(parts, axis=1) if len(parts) > 1 else parts[0]
        elif kind == "rope256":
            parts = []
            for g in range(width // 256):
                x1 = y[:, g * 256:g * 256 + LANES]
                x2 = y[:, g * 256 + LANES:(g + 1) * 256]
                parts += [x1 * cos - x2 * sin, x1 * sin + x2 * cos]
            y = jnp.concatenate(parts, axis=1)
        if scale != 1.0:
            y = y * scale
        o_ref[:, c0:c0 + width] = y.astype(o_ref.dtype)
        if block_mean is not None and block_mean[0] <= c0 < block_mean[1]:
            km_ref = out_refs[1]
            blk = block_mean[2]
            for r in range(tm // blk):
                km_ref[r, :, c0 - block_mean[0]:c0 - block_mean[0] + width] = jnp.mean(
                    y[r * blk:(r + 1) * blk], axis=0, keepdims=True)
    if side is not None:
        c0, width, scale = side
        out_refs[-1][...] = jnp.dot(xb, w_ref[:, c0:c0 + width], preferred_element_type=F32) * scale


def _proj(x2d, w, cos, sin, seq, chunks, out_dtype, block_mean=None, side=None):
    t, d = x2d.shape
    n = w.shape[1]
    tm = ROW_TILE
    tiles_per_seq = seq // tm
    out_shape = [jax.ShapeDtypeStruct((t, n), out_dtype)]
    out_specs = [pl.BlockSpec((tm, n), lambda i: (i, 0))]
    if block_mean is not None:
        lo, hi, blk = block_mean
        out_shape.append(jax.ShapeDtypeStruct((t // blk, 1, hi - lo), F32))
        out_specs.append(pl.BlockSpec((tm // blk, 1, hi - lo), lambda i: (i, 0, 0)))
    if side is not None:
        out_shape.append(jax.ShapeDtypeStruct((t, side[1]), F32))
        out_specs.append(pl.BlockSpec((tm, side[1]), lambda i: (i, 0)))
    res = pl.pallas_call(
        functools.partial(_proj_body, chunks=tuple(chunks), block_mean=block_mean, side=side),
        grid=(t // tm,),
        in_specs=[
            pl.BlockSpec((tm, d), lambda i: (i, 0)),
            pl.BlockSpec((d, n), lambda i: (0, 0)),
            pl.BlockSpec((tm, LANES), lambda i: (i % tiles_per_seq, 0)),
            pl.BlockSpec((tm, LANES), lambda i: (i % tiles_per_seq, 0)),
        ],
        out_specs=out_specs,
        out_shape=out_shape,
        compiler_params=_params("parallel"),
    )(x2d, w, cos, sin)
    return res if len(res) > 1 else res[0]


def _out_ln_body(a_ref, w_ref, x_ref, g_ref, b_ref, o_ref):
    f = jnp.dot(a_ref[...], w_ref[...], preferred_element_type=F32)
    y = ALPHA * x_ref[...] + f
    o_ref[...] = _layer_norm_rows(y, g_ref[...], b_ref[...])


def _out_ln(a, w, x2d, g, b):
    t, kin = a.shape
    d = x2d.shape[1]
    tm = ROW_TILE
    return pl.pallas_call(
        _out_ln_body,
        grid=(t // tm,),
        in_specs=[
            pl.BlockSpec((tm, kin), lambda i: (i, 0)),
            pl.BlockSpec((kin, d), lambda i: (0, 0)),
            pl.BlockSpec((tm, d), lambda i: (i, 0)),
            pl.BlockSpec((1, d), lambda i: (0, 0)),
            pl.BlockSpec((1, d), lambda i: (0, 0)),
        ],
        out_specs=pl.BlockSpec((tm, d), lambda i: (i, 0)),
        out_shape=jax.ShapeDtypeStruct((t, d), F32),
        compiler_params=_params("parallel"),
    )(a, w, x2d, g.reshape(1, d), b.reshape(1, d))


def _ffn_body(x_ref, halo_ref, wup_ref, cw_ref, cb_ref, wdn_ref, g_ref, b_ref, o_ref,
              xe_ref, h_ref, act_ref, *, tiles_per_seq):
    i = pl.program_id(0)
    tm = x_ref.shape[0]
    seq_start = (i % tiles_per_seq) == 0
    halo = halo_ref[...]
    xe_ref[0:HALO, :] = jnp.where(seq_start, 0.0, halo).astype(BF16)
    xe_ref[HALO:, :] = x_ref[...].astype(BF16)
    xe = xe_ref[...]

    def conv(c0, slot):
        h = h_ref.at[slot]
        h[...] = jnp.dot(xe, wup_ref[:, c0:c0 + FFN_CHUNK], preferred_element_type=F32)
        w = cw_ref[:, c0:c0 + FFN_CHUNK]
        return (h[HALO - 2:HALO - 2 + tm, :] * w[0:1, :]
                + h[HALO - 1:HALO - 1 + tm, :] * w[1:2, :]
                + h[HALO:HALO + tm, :] * w[2:3, :]
                + cb_ref[:, c0:c0 + FFN_CHUNK])

    for c in range(D_FF // FFN_CHUNK):
        c0 = c * FFN_CHUNK
        gate = conv(c0, 2 * (c % 2))
        up = conv(D_FF + c0, 2 * (c % 2) + 1)
        act_ref[:, c0:c0 + FFN_CHUNK] = (gate * jax.nn.sigmoid(gate) * up).astype(BF16)
    f = jnp.dot(act_ref[...], wdn_ref[...], preferred_element_type=F32)
    y = ALPHA * x_ref[...] + f
    o_ref[...] = _layer_norm_rows(y, g_ref[...], b_ref[...])


def _ffn_ln(x2d, seq, w_up, conv_w, conv_b, w_down, g, b):
    t, d = x2d.shape
    tm = FFN_ROW_TILE
    tiles_per_seq = seq // tm
    halo_blocks = tm // HALO
    resident = pl.Buffered(1)
    return pl.pallas_call(
        functools.partial(_ffn_body, tiles_per_seq=tiles_per_seq),
        grid=(t // tm,),
        in_specs=[
            pl.BlockSpec((tm, d), lambda i: (i, 0)),
            pl.BlockSpec((HALO, d), lambda i: (jnp.maximum(i * halo_blocks - 1, 0), 0)),
            pl.BlockSpec((d, 2 * D_FF), lambda i: (0, 0), pipeline_mode=resident),
            pl.BlockSpec((3, 2 * D_FF), lambda i: (0, 0)),
            pl.BlockSpec((1, 2 * D_FF), lambda i: (0, 0)),
            pl.BlockSpec((D_FF, d), lambda i: (0, 0), pipeline_mode=resident),
            pl.BlockSpec((1, d), lambda i: (0, 0)),
            pl.BlockSpec((1, d), lambda i: (0, 0)),
        ],
        out_specs=pl.BlockSpec((tm, d), lambda i: (i, 0)),
        out_shape=jax.ShapeDtypeStruct((t, d), F32),
        scratch_shapes=[
            pltpu.VMEM((tm + HALO, d), BF16),
            pltpu.VMEM((4, tm + HALO, FFN_CHUNK), F32),
            pltpu.VMEM((tm, D_FF), BF16),
        ],
        compiler_params=_params("parallel"),
    )(x2d, x2d, w_up.astype(BF16), conv_w, conv_b.reshape(1, -1), w_down.astype(BF16),
      g.reshape(1, d), b.reshape(1, d))


def _sortable_key(score):
    bits = lax.bitcast_convert_type(score, I32)
    key = jnp.where(bits < 0, bits ^ jnp.int32(0x7FFFFFFF), bits)
    return jnp.where(score == 0.0, 0, key)


def _dsa_body(q_ref, qi_ref, wi_ref, k_ref, ki_ref, v_ref, o_ref,
              key_ref, lhs_ref, wb_ref, qm_ref, cmp_ref, thr_ref, s_ref, p_ref, m_ref, acc_ref,
              *, topk):
    qn, tk = DSA_Q, DSA_TK
    groups = tk // LANES
    i = pl.program_id(1)
    nkt = (i * qn + qn + tk - 1) // tk
    lane = lax.broadcasted_iota(I32, (qn, LANES), 1)
    low = lane < 64
    row_pos = i * qn + lax.broadcasted_iota(I32, (qn, LANES), 0)

    for p in range(IDX_HEADS // 2):
        pair = qi_ref[:, p * LANES:(p + 1) * LANES]
        lhs_ref[2 * p * qn:(2 * p + 1) * qn, :] = jnp.where(low, pair, jnp.zeros_like(pair))
        lhs_ref[(2 * p + 1) * qn:(2 * p + 2) * qn, :] = jnp.where(low, jnp.zeros_like(pair), pair)
    for h in range(IDX_HEADS):
        wb_ref[h] = jnp.broadcast_to(wi_ref[:, h:h + 1], (qn, LANES))

    def index_tile(kt, carry):
        ki_t = ki_ref[pl.ds(pl.multiple_of(kt * tk, tk), tk), :]
        lg_all = _nt_dot(lhs_ref[...], ki_t)
        acc = [jnp.zeros((qn, LANES), F32) for _ in range(groups)]
        for h in range(IDX_HEADS):
            lg = lg_all[h * qn:(h + 1) * qn]
            wb = wb_ref[h]
            for g in range(groups):
                acc[g] = acc[g] + wb * jnp.maximum(lg[:, g * LANES:(g + 1) * LANES], 0.0)
        for g in range(groups):
            col = kt * tk + g * LANES + lane
            key_ref[kt, :, g * LANES:(g + 1) * LANES] = jnp.where(
                col <= row_pos, _sortable_key(acc[g]), INT_MIN)
        return carry

    lax.fori_loop(0, nkt, index_tile, 0)

    def count(pred):
        def body(kt, acc):
            ref_val = cmp_ref[...]
            for g in range(groups):
                keys = key_ref[kt, :, g * LANES:(g + 1) * LANES]
                acc = acc + jnp.where(pred(keys, ref_val, kt * tk + g * LANES + lane), 1.0, 0.0)
            return acc
        acc = lax.fori_loop(0, nkt, body, jnp.zeros((qn, LANES), F32))
        return jnp.sum(acc, axis=-1, keepdims=True)

    def bit_step(it, t_u):
        cand_u = t_u | jnp.left_shift(jnp.int32(1), 31 - it)
        cmp_ref[...] = jnp.broadcast_to(cand_u ^ INT_MIN, (qn, LANES))
        cnt = count(lambda keys, cand, pos: keys >= cand)
        return jnp.where(cnt >= topk, cand_u, t_u)

    t_u = lax.fori_loop(0, 32, bit_step, jnp.zeros((qn, 1), I32))
    thr1 = jnp.maximum(t_u ^ INT_MIN, INT_MIN + 1)
    thr_ref[...] = jnp.broadcast_to(thr1, (qn, LANES))
    cmp_ref[...] = thr_ref[...]
    cnt_gt = count(lambda keys, thr, pos: keys > thr)
    cnt_eq = count(lambda keys, thr, pos: keys == thr)
    need = topk - cnt_gt
    excess_ties = jnp.max(jnp.where(cnt_eq > need, 1.0, 0.0)) > 0.0

    @pl.when(excess_ties)
    def _():
        pos_bits = max(1, int(np.ceil(np.log2(key_ref.shape[0] * tk))))

        def pos_step(it, p_lim):
            cand1 = p_lim | jnp.left_shift(jnp.int32(1), pos_bits - 1 - it)
            cmp_ref[...] = jnp.broadcast_to(cand1, (qn, LANES))
            cnt = count(lambda keys, cand, pos: (keys == thr_ref[...]) & (pos < cand))
            return jnp.where(cnt < need, cand1, p_lim)

        p_lim = jnp.broadcast_to(
            lax.fori_loop(0, pos_bits, pos_step, jnp.zeros((qn, 1), I32)), (qn, LANES))

        def demote(kt, carry):
            thr = thr_ref[...]
            for g in range(groups):
                keys = key_ref[kt, :, g * LANES:(g + 1) * LANES]
                pos = kt * tk + g * LANES + lane
                key_ref[kt, :, g * LANES:(g + 1) * LANES] = jnp.where(
                    (keys == thr) & (pos > p_lim), thr - 1, keys)
            return carry

        lax.fori_loop(0, nkt, demote, 0)

    half = DSA_HEADS // 2
    eye = jnp.where(lax.broadcasted_iota(I32, (qn, LANES), 0) == lane, 1.0, 0.0).astype(BF16)
    for p in range(half):
        pair = q_ref[:, p * LANES:(p + 1) * LANES]
        zero = jnp.zeros_like(pair)
        qm_ref[p * qn:(p + 1) * qn, 0:LANES] = jnp.where(low, pair, zero)
        qm_ref[(half + p) * qn:(half + p + 1) * qn, 0:LANES] = jnp.where(low, zero, pair)
    for h in range(DSA_HEADS):
        qm_ref[h * qn:(h + 1) * qn, LANES:2 * LANES] = eye
    m_ref[...] = jnp.full(m_ref.shape, NEG, F32)
    acc_ref[...] = jnp.zeros(acc_ref.shape, F32)
    low_k = lax.broadcasted_iota(I32, (tk, LANES), 1) < 64

    def scores(kt_raw):
        kt = jnp.minimum(kt_raw, nkt - 1)
        thr_t = jnp.where(kt_raw < nkt, thr_ref[...], jnp.int32(2 ** 31 - 1))
        bias = jnp.concatenate(
            [jnp.where(key_ref[kt, :, g * LANES:(g + 1) * LANES] >= thr_t, 0.0, NEG)
             for g in range(groups)], axis=1)
        k_aug = jnp.concatenate(
            [k_ref[pl.ds(pl.multiple_of(kt * tk, tk), tk), :], bias.T.astype(BF16)], axis=1)
        return _nt_dot(qm_ref[...], k_aug)

    def attend_tile(kt_raw, cur):
        s_ref[cur] = scores(kt_raw)
        kt = jnp.minimum(kt_raw, nkt - 1)
        v_t = v_ref[pl.ds(pl.multiple_of(kt * tk, tk), tk), :]
        one = jnp.ones_like(v_t)
        for h in range(DSA_HEADS):
            rows = slice(h * qn, (h + 1) * qn)
            sg = [s_ref[cur, rows, g * LANES:(g + 1) * LANES] for g in range(groups)]
            m_new = jnp.maximum(m_ref[1 - cur, h],
                                jnp.max(functools.reduce(jnp.maximum, sg), axis=-1, keepdims=True))
            m_ref[cur, h] = m_new
            for g in range(groups):
                p_ref[rows, g * LANES:(g + 1) * LANES] = jnp.exp2(sg[g] - m_new).astype(BF16)
        pv = (jnp.dot(p_ref[0:half * qn, :], jnp.where(low_k, v_t, one), preferred_element_type=F32),
              jnp.dot(p_ref[half * qn:, :], jnp.where(low_k, one, v_t), preferred_element_type=F32))
        for h in range(DSA_HEADS):
            alpha = jnp.exp2(m_ref[1 - cur, h] - m_ref[cur, h])
            acc_ref[h] = alpha * acc_ref[h] + pv[h // half][(h % half) * qn:(h % half + 1) * qn]

    def attend_pair(t, carry):
        attend_tile(2 * t, 0)
        attend_tile(2 * t + 1, 1)
        return carry

    lax.fori_loop(0, (nkt + 1) // 2, attend_pair, 0)

    for p in range(half):
        a0 = acc_ref[p]
        a1 = acc_ref[half + p]
        o0 = a0 / pltpu.roll(a0, 64, 1)
        o1 = a1 / pltpu.roll(a1, 64, 1)
        o_ref[:, p * LANES:(p + 1) * LANES] = jnp.where(low, o0, o1).astype(o_ref.dtype)


def _dsa_attention(proj, wi, bsz, seq, topk):
    t = bsz * seq
    qn, tk = DSA_Q, DSA_TK
    nqb = seq // qn
    nkt_max = seq // tk
    qcols = DSA_HEADS * DSA_HEAD_DIM
    icols = IDX_HEADS * IDX_DIM
    kcol = (qcols + icols) // LANES
    return pl.pallas_call(
        functools.partial(_dsa_body, topk=topk),
        grid=(bsz, nqb),
        in_specs=[
            pl.BlockSpec((qn, qcols), lambda b, i: (b * nqb + i, 0)),
            pl.BlockSpec((qn, icols), lambda b, i: (b * nqb + i, qcols // icols)),
            pl.BlockSpec((qn, LANES), lambda b, i: (b * nqb + i, 0)),
            pl.BlockSpec((seq, LANES), lambda b, i: (b, kcol)),
            pl.BlockSpec((seq, LANES), lambda b, i: (b, kcol + 1)),
            pl.BlockSpec((seq, LANES), lambda b, i: (b, kcol + 2)),
        ],
        out_specs=pl.BlockSpec((qn, qcols), lambda b, i: (b * nqb + i, 0)),
        out_shape=jax.ShapeDtypeStruct((t, qcols), BF16),
        scratch_shapes=[
            pltpu.VMEM((nkt_max, qn, tk), I32),
            pltpu.VMEM((IDX_HEADS * qn, LANES), BF16),
            pltpu.VMEM((IDX_HEADS, qn, LANES), F32),
            pltpu.VMEM((DSA_HEADS * qn, 2 * LANES), BF16),
            pltpu.VMEM((qn, LANES), I32),
            pltpu.VMEM((qn, LANES), I32),
            pltpu.VMEM((2, DSA_HEADS * qn, tk), F32),
            pltpu.VMEM((DSA_HEADS * qn, tk), BF16),
            pltpu.VMEM((2, DSA_HEADS, qn, LANES), F32),
            pltpu.VMEM((DSA_HEADS, qn, LANES), F32),
        ],
        compiler_params=_params("parallel", "arbitrary"),
    )(proj, proj, wi, proj, proj, proj)


def _dsa_layer(x2d, bsz, seq, w_in, w_out, ln_g, ln_b):
    h, dh, hi, di = DSA_HEADS, DSA_HEAD_DIM, IDX_HEADS, IDX_DIM
    offs = np.cumsum([h * dh, dh, dh, hi * di, di, hi])
    wq, wk, wv, wqi, wki, wwi = (w_in[:, a:b] for a, b in zip([0] + offs[:-1].tolist(), offs.tolist()))
    wwi_pad = jnp.concatenate([wwi, jnp.zeros((D_MODEL, LANES - hi), F32)], axis=1)
    w_main = jnp.concatenate([wq, wqi, wk, wk, wki, wki, wv, wv, wwi_pad], axis=1).astype(BF16)
    cos, sin = _rope_tables(seq, dh)
    scale = dh ** -0.5 * LOG2E
    chunks = [(0, 512, "rope64", scale), (512, 512, "rope64", scale), (1024, 512, "rope64", 1.0),
              (1536, 256, "rope64", 1.0), (1792, 256, "plain", 1.0)]
    proj, wi = _proj(x2d, w_main, cos, sin, seq, chunks, BF16,
                     side=(w_main.shape[1] - LANES, LANES, hi ** -0.5 * di ** -0.5))
    o = _dsa_attention(proj, wi, bsz, seq, min(DSA_TOPK, seq // 4))
    return _out_ln(o, w_out.astype(BF16), x2d, ln_g, ln_b)


def _moba_body(q_ref, k_ref, v_ref, km_ref, o_ref, qa_ref, m_ref, acc_ref, *, n_sel):
    blk = MOBA_BLOCK
    nq = MOBA_QBLOCKS_PER_STEP
    rows_n = nq * blk
    first_blk = pl.program_id(2) * nq
    lane = lax.broadcasted_iota(I32, (rows_n, LANES), 1)
    low = lane < 64
    own_blk = first_blk + lax.broadcasted_iota(I32, (rows_n, LANES), 0) // blk

    def own(h, mine, other):
        m = low if mine.shape[0] == rows_n else low_k
        return jnp.where(m, mine, other) if h == 0 else jnp.where(m, other, mine)

    zero = jnp.zeros((rows_n, LANES), BF16)
    one_k = jnp.ones((blk, LANES), BF16)
    r_loc = lax.broadcasted_iota(I32, (blk, blk), 0)
    c_loc = lax.broadcasted_iota(I32, (blk, blk), 1)
    lane_f = lane.astype(F32)
    lane_k = lax.broadcasted_iota(I32, (blk, LANES), 1)
    low_k = lane_k < 64
    heads = 2 * MOBA_PAIRS_PER_STEP

    def pair_lanes(hd):
        return slice((hd // 2) * LANES, (hd // 2 + 1) * LANES)

    def key_block(hd, j, with_block_id):
        h = hd % 2
        k_t = k_ref[pl.ds(pl.multiple_of(j * blk, blk), blk), pair_lanes(hd)]
        if not with_block_id:
            return k_t
        block_lane = jnp.where(lane_k == (j + 64 if h == 0 else j), 1.0, 0.0).astype(BF16)
        return own(h, k_t, block_lane)

    def value_block(hd, j):
        return own(hd % 2, v_ref[pl.ds(pl.multiple_of(j * blk, blk), blk), pair_lanes(hd)], one_k)

    for hd in range(heads):
        h = hd % 2
        pair = q_ref[:, pair_lanes(hd)]
        km = km_ref[0, :, pair_lanes(hd)].astype(BF16)
        qh = own(h, pair, zero)
        gate = jnp.where(lane < own_blk, _nt_dot(qh, km), NEG)
        chosen = lane < 0
        for _ in range(n_sel):
            best = jnp.max(gate, axis=-1, keepdims=True)
            first = jnp.min(jnp.where(gate == best, lane_f, float(LANES)), axis=-1, keepdims=True)
            hit = lane_f == first
            chosen = jnp.logical_or(chosen, hit)
            gate = jnp.where(hit, -jnp.inf, gate)
        pen = jnp.where(jnp.logical_and(chosen, lane < own_blk), 0.0, NEG)
        if h == 0:
            pen = pltpu.roll(pen, 64, 1)
        qa = own(h, pair, pen.astype(BF16))
        qa_ref[hd] = qa

        for a in range(nq):
            rs = slice(a * blk, (a + 1) * blk)
            s = [_nt_dot(qa[rs], key_block(hd, first_blk + bb, True)) for bb in range(a)]
            s.append(jnp.where(c_loc <= r_loc, _nt_dot(qh[rs], key_block(hd, first_blk + a, False)), NEG))
            m0 = jnp.max(functools.reduce(jnp.maximum, s), axis=-1, keepdims=True)
            m_ref[hd, rs, :] = jnp.broadcast_to(m0, (blk, LANES))
            acc_ref[hd, rs, :] = functools.reduce(jnp.add, [
                jnp.dot(jnp.exp2(s[bb] - m0).astype(BF16), value_block(hd, first_blk + bb),
                        preferred_element_type=F32) for bb in range(a + 1)])

    def past_pair(t, carry):
        for hd in range(heads):
            s = [_nt_dot(qa_ref[hd], key_block(hd, 2 * t + u, True)) for u in range(2)]
            m_old = m_ref[hd]
            m_new = jnp.maximum(m_old, jnp.max(jnp.maximum(s[0], s[1]), axis=-1, keepdims=True))
            alpha = jnp.exp2(m_old - m_new)
            m_ref[hd] = m_new
            pv = [jnp.dot(jnp.exp2(s[u] - m_new[:, 0:1]).astype(BF16), value_block(hd, 2 * t + u),
                          preferred_element_type=F32) for u in range(2)]
            acc_ref[hd] = alpha * acc_ref[hd] + (pv[0] + pv[1])
        return carry

    lax.fori_loop(0, first_blk // 2, past_pair, 0)
    for pp in range(MOBA_PAIRS_PER_STEP):
        a0 = acc_ref[2 * pp]
        a1 = acc_ref[2 * pp + 1]
        o_ref[:, pp * LANES:(pp + 1) * LANES] = jnp.where(
            low, a0 / pltpu.roll(a0, 64, 1), a1 / pltpu.roll(a1, 64, 1)).astype(o_ref.dtype)


def _moba_attention(proj, kmean, bsz, seq):
    t = bsz * seq
    rows_n = MOBA_QBLOCKS_PER_STEP * MOBA_BLOCK
    nsb = seq // rows_n
    width = MOBA_PAIRS_PER_STEP * LANES
    groups = MOBA_HEADS * MOBA_HEAD_DIM // width
    heads = 2 * MOBA_PAIRS_PER_STEP
    n_sel = min(MOBA_TOPK, seq // MOBA_BLOCK - 1)
    return pl.pallas_call(
        functools.partial(_moba_body, n_sel=n_sel),
        grid=(bsz, groups, nsb),
        in_specs=[
            pl.BlockSpec((rows_n, width), lambda b, p, i: (b * nsb + i, p)),
            pl.BlockSpec((seq, width), lambda b, p, i: (b, groups + p)),
            pl.BlockSpec((seq, width), lambda b, p, i: (b, 2 * groups + p)),
            pl.BlockSpec((1, LANES, width), lambda b, p, i: (b, 0, p)),
        ],
        out_specs=pl.BlockSpec((rows_n, width), lambda b, p, i: (b * nsb + i, p)),
        out_shape=jax.ShapeDtypeStruct((t, MOBA_HEADS * MOBA_HEAD_DIM), BF16),
        scratch_shapes=[
            pltpu.VMEM((heads, rows_n, LANES), BF16),
            pltpu.VMEM((heads, rows_n, LANES), F32),
            pltpu.VMEM((heads, rows_n, LANES), F32),
        ],
        compiler_params=_params("parallel", "parallel", "arbitrary"),
    )(proj, proj, proj, kmean)


def _moba_layer(x2d, bsz, seq, w_in, w_out, ln_g, ln_b):
    hd = MOBA_HEADS * MOBA_HEAD_DIM
    cos, sin = _rope_tables(seq, MOBA_HEAD_DIM)
    scale = MOBA_HEAD_DIM ** -0.5 * LOG2E
    chunks = [(c0, 512, "rope64", scale if c0 < hd else 1.0) for c0 in range(0, 2 * hd, 512)]
    chunks += [(c0, 512, "plain", 1.0) for c0 in range(2 * hd, 3 * hd, 512)]
    proj, kmean = _proj(x2d, w_in.astype(BF16), cos, sin, seq, chunks, BF16,
                        block_mean=(hd, 2 * hd, MOBA_BLOCK))
    nb = seq // MOBA_BLOCK
    kmean = jnp.pad(kmean.reshape(bsz, nb, hd), ((0, 0), (0, LANES - nb), (0, 0)))
    o = _moba_attention(proj, kmean, bsz, seq)
    return _out_ln(o, w_out.astype(BF16), x2d, ln_g, ln_b)


def _ret_body(q_ref, k_ref, v_ref, g_ref, lg_ref, gng_ref, gnb_ref, o_ref, state_ref):
    c = pl.program_id(2)
    cc = RET_CHUNK

    @pl.when(c == 0)
    def _():
        state_ref[...] = jnp.zeros(state_ref.shape, F32)

    lg = lg_ref[0]
    ri = lax.broadcasted_iota(I32, (cc, cc), 0)
    ci = lax.broadcasted_iota(I32, (cc, cc), 1)
    rel = (ri - ci).astype(F32)
    lg_row = jnp.concatenate([lg] * (cc // LANES), axis=1)
    decay = jnp.where(rel >= 0, jnp.exp(jnp.maximum(rel, 0.0) * lg_row), 0.0)
    pos = lax.broadcasted_iota(I32, (cc, LANES), 0).astype(F32)
    q_decay = jnp.exp((pos + 1.0) * lg)[:, 0:1]
    k_decay = jnp.exp((cc - 1.0 - pos) * lg)[:, 0:1]
    chunk_decay = jnp.exp(cc * lg)[:, 0:1]

    q = q_ref[...]
    k = k_ref[...]
    v = v_ref[...]
    state = state_ref[...]
    inner = jnp.dot((_nt_dot(q, k) * decay).astype(BF16), v, preferred_element_type=F32)
    cross = jnp.dot(q, state.astype(BF16), preferred_element_type=F32) * q_decay
    kd_t = (k.astype(F32) * k_decay).T.astype(BF16)
    state_ref[...] = state * chunk_decay + jnp.dot(kd_t, v, preferred_element_type=F32)

    o = inner + cross
    mu = jnp.mean(o, axis=-1, keepdims=True)
    d = o - mu
    var = jnp.mean(d * d, axis=-1, keepdims=True)
    on = d * lax.rsqrt(var + GN_EPS) * gng_ref[...] + gnb_ref[...]
    gate = g_ref[...]
    o_ref[...] = (gate * jax.nn.sigmoid(gate) * on).astype(o_ref.dtype)


def _retention(proj, gate, lg, gn_g, gn_b, bsz, seq):
    t = bsz * seq
    cc = RET_CHUNK
    nc = seq // cc
    h, dk, dv = RET_HEADS, RET_QK_DIM, RET_V_DIM
    return pl.pallas_call(
        _ret_body,
        grid=(bsz, h, nc),
        in_specs=[
            pl.BlockSpec((cc, dk), lambda b, hh, c: (b * nc + c, hh)),
            pl.BlockSpec((cc, dk), lambda b, hh, c: (b * nc + c, h + hh)),
            pl.BlockSpec((cc, dv), lambda b, hh, c: (b * nc + c, (2 * h * dk) // dv + hh)),
            pl.BlockSpec((cc, dv), lambda b, hh, c: (b * nc + c, hh)),
            pl.BlockSpec((1, 1, LANES), lambda b, hh, c: (hh, 0, 0)),
            pl.BlockSpec((1, dv), lambda b, hh, c: (0, hh)),
            pl.BlockSpec((1, dv), lambda b, hh, c: (0, hh)),
        ],
        out_specs=pl.BlockSpec((cc, dv), lambda b, hh, c: (b * nc + c, hh)),
        out_shape=jax.ShapeDtypeStruct((t, h * dv), BF16),
        scratch_shapes=[pltpu.VMEM((dk, dv), F32)],
        compiler_params=_params("parallel", "parallel", "arbitrary"),
    )(proj, proj, proj, gate, lg, gn_g.reshape(1, -1), gn_b.reshape(1, -1))


def _retention_layer(x2d, bsz, seq, w_in, gn_g, gn_b, w_out, ln_g, ln_b):
    h, dk, dv = RET_HEADS, RET_QK_DIM, RET_V_DIM
    cos, sin = _rope_tables(seq, dk)
    nqk = 2 * h * dk
    chunks = [(c0, 512, "rope256", 1.0 if c0 < h * dk else dk ** -0.5) for c0 in range(0, nqk, 512)]
    chunks += [(c0, 512, "plain", 1.0) for c0 in range(nqk, nqk + h * dv, 512)]
    proj = _proj(x2d, w_in[:, :nqk + h * dv].astype(BF16), cos, sin, seq, chunks, BF16)
    gchunks = [(c0, 512, "plain", 1.0) for c0 in range(0, h * dv, 512)]
    gate = _proj(x2d, w_in[:, nqk + h * dv:].astype(BF16), cos, sin, seq, gchunks, F32)
    log_gamma = jnp.log1p(-(2.0 ** (-5.0 - jnp.arange(h, dtype=F32))))
    lg = jnp.broadcast_to(log_gamma[:, None, None], (h, 1, LANES))
    y = _retention(proj, gate, lg, gn_g, gn_b, bsz, seq)
    return _out_ln(y, w_out.astype(BF16), x2d, ln_g, ln_b)


def kernel(x, l0_w_in, l0_w_out, l0_ln1_g, l0_ln1_b, l0_w_up, l0_conv_w, l0_conv_b, l0_w_down, l0_ln2_g, l0_ln2_b, l1_w_in, l1_w_out, l1_ln1_g, l1_ln1_b, l1_w_up, l1_conv_w, l1_conv_b, l1_w_down, l1_ln2_g, l1_ln2_b, l2_w_in, l2_gn_g, l2_gn_b, l2_w_out, l2_ln1_g, l2_ln1_b, l2_w_up, l2_conv_w, l2_conv_b, l2_w_down, l2_ln2_g, l2_ln2_b, l3_w_in, l3_w_out, l3_ln1_g, l3_ln1_b, l3_w_up, l3_conv_w, l3_conv_b, l3_w_down, l3_ln2_g, l3_ln2_b):
    bsz, seq, d = x.shape
    h = x.reshape(bsz * seq, d)
    h = _dsa_layer(h, bsz, seq, l0_w_in, l0_w_out, l0_ln1_g, l0_ln1_b)
    h = _ffn_ln(h, seq, l0_w_up, l0_conv_w, l0_conv_b, l0_w_down, l0_ln2_g, l0_ln2_b)
    h = _moba_layer(h, bsz, seq, l1_w_in, l1_w_out, l1_ln1_g, l1_ln1_b)
    h = _ffn_ln(h, seq, l1_w_up, l1_conv_w, l1_conv_b, l1_w_down, l1_ln2_g, l1_ln2_b)
    h = _retention_layer(h, bsz, seq, l2_w_in, l2_gn_g, l2_gn_b, l2_w_out, l2_ln1_g, l2_ln1_b)
    h = _ffn_ln(h, seq, l2_w_up, l2_conv_w, l2_conv_b, l2_w_down, l2_ln2_g, l2_ln2_b)
    h = _dsa_layer(h, bsz, seq, l3_w_in, l3_w_out, l3_ln1_g, l3_ln1_b)
    h = _ffn_ln(h, seq, l3_w_up, l3_conv_w, l3_conv_b, l3_w_down, l3_ln2_g, l3_ln2_b)
    return h.reshape(bsz, seq, d)
```

```python
import functools

import jax
import jax.numpy as jnp
import numpy as np
from jax import lax
from jax.experimental import pallas as pl
from jax.experimental.pallas import tpu as pltpu

F32 = jnp.float32
BF16 = jnp.bfloat16
I32 = jnp.int32

LANES = 128
VMEM_LIMIT_BYTES = 56 * 1024 * 1024

D_MODEL = 1024
DEPTH = 4
ROPE_THETA = 10000.0
ALPHA = (2 * DEPTH) ** 0.25
LN_EPS = 1e-5
GN_EPS = 1e-6
NEG = -1e30
INT_MIN = -(2 ** 31)
LOG2E = 1.4426950408889634

DSA_HEADS = 16
DSA_HEAD_DIM = 64
IDX_HEADS = 8
IDX_DIM = 64
DSA_TOPK = 256
DSA_Q = 128
DSA_TK = 512

MOBA_HEADS = 16
MOBA_HEAD_DIM = 64
MOBA_BLOCK = 256
MOBA_TOPK = 3
MOBA_PAIRS_PER_STEP = 2
MOBA_QBLOCKS_PER_STEP = 4

RET_HEADS = 4
RET_QK_DIM = 256
RET_V_DIM = 512
RET_CHUNK = 256

D_FF = 2816
FFN_CHUNK = 256
ROW_TILE = 512
FFN_ROW_TILE = 512
HALO = 16


def _params(*sem):
    return pltpu.CompilerParams(dimension_semantics=sem, vmem_limit_bytes=VMEM_LIMIT_BYTES)


def _nt_dot(a, b):
    return lax.dot_general(a, b, (((1,), (1,)), ((), ())), preferred_element_type=F32)


def _rope_tables(seq, dim):
    half = dim // 2
    inv = 1.0 / (ROPE_THETA ** (jnp.arange(0, dim, 2, dtype=F32) / dim))
    ang = jnp.arange(seq, dtype=F32)[:, None] * inv[None, :]
    cos, sin = jnp.cos(ang), jnp.sin(ang)
    if half >= LANES:
        return cos, sin
    reps = LANES // dim
    cos_t = jnp.tile(jnp.concatenate([cos, cos], axis=1), (1, reps))
    sin_t = jnp.tile(jnp.concatenate([-sin, sin], axis=1), (1, reps))
    return cos_t, sin_t


def _layer_norm_rows(y, g, b):
    mu = jnp.mean(y, axis=-1, keepdims=True)
    d = y - mu
    var = jnp.mean(d * d, axis=-1, keepdims=True)
    return d * lax.rsqrt(var + LN_EPS) * g + b


def _proj_body(x_ref, w_ref, cos_ref, sin_ref, *out_refs, chunks, block_mean, side):
    o_ref = out_refs[0]
    xb = x_ref[...].astype(BF16)
    tm = xb.shape[0]
    cos = cos_ref[...]
    sin = sin_ref[...]
    lane = lax.broadcasted_iota(I32, (tm, LANES), 1)
    first_half = (lane % 64) < 32
    for c0, width, kind, scale in chunks:
        y = jnp.dot(xb, w_ref[:, c0:c0 + width], preferred_element_type=F32)
        if kind == "rope64":
            parts = []
            for g in range(width // LANES):
                yg = y[:, g * LANES:(g + 1) * LANES]
                rot = jnp.where(first_half, pltpu.roll(yg, LANES - 32, 1), pltpu.roll(yg, 32, 1))
                parts.append(yg * cos + rot * sin)
            y = jnp.concatenate(parts, axis=1) if len(parts) > 1 else parts[0]
        elif kind == "rope256":
            parts = []
            for g in range(width // 256):
                x1 = y[:, g * 256:g * 256 + LANES]
                x2 = y[:, g * 256 + LANES:(g + 1) * 256]
                parts += [x1 * cos - x2 * sin, x1 * sin + x2 * cos]
            y = jnp.concatenate(parts, axis=1)
        if scale != 1.0:
            y = y * scale
        o_ref[:, c0:c0 + width] = y.astype(o_ref.dtype)
        if block_mean is not None and block_mean[0] <= c0 < block_mean[1]:
            km_ref = out_refs[1]
            blk = block_mean[2]
            for r in range(tm // blk):
                km_ref[r, :, c0 - block_mean[0]:c0 - block_mean[0] + width] = jnp.mean(
                    y[r * blk:(r + 1) * blk], axis=0, keepdims=True)
    if side is not None:
        c0, width, scale = side
        out_refs[-1][...] = jnp.dot(xb, w_ref[:, c0:c0 + width], preferred_element_type=F32) * scale


def _proj(x2d, w, cos, sin, seq, chunks, out_dtype, block_mean=None, side=None):
    t, d = x2d.shape
    n = w.shape[1]
    tm = ROW_TILE
    tiles_per_seq = seq // tm
    out_shape = [jax.ShapeDtypeStruct((t, n), out_dtype)]
    out_specs = [pl.BlockSpec((tm, n), lambda i: (i, 0))]
    if block_mean is not None:
        lo, hi, blk = block_mean
        out_shape.append(jax.ShapeDtypeStruct((t // blk, 1, hi - lo), F32))
        out_specs.append(pl.BlockSpec((tm // blk, 1, hi - lo), lambda i: (i, 0, 0)))
    if side is not None:
        out_shape.append(jax.ShapeDtypeStruct((t, side[1]), F32))
        out_specs.append(pl.BlockSpec((tm, side[1]), lambda i: (i, 0)))
    res = pl.pallas_call(
        functools.partial(_proj_body, chunks=tuple(chunks), block_mean=block_mean, side=side),
        grid=(t // tm,),
        in_specs=[
            pl.BlockSpec((tm, d), lambda i: (i, 0)),
            pl.BlockSpec((d, n), lambda i: (0, 0)),
            pl.BlockSpec((tm, LANES), lambda i: (i % tiles_per_seq, 0)),
            pl.BlockSpec((tm, LANES), lambda i: (i % tiles_per_seq, 0)),
        ],
        out_specs=out_specs,
        out_shape=out_shape,
        compiler_params=_params("parallel"),
    )(x2d, w, cos, sin)
    return res if len(res) > 1 else res[0]


def _out_ln_body(a_ref, w_ref, x_ref, g_ref, b_ref, o_ref):
    f = jnp.dot(a_ref[...], w_ref[...], preferred_element_type=F32)
    y = ALPHA * x_ref[...] + f
    o_ref[...] = _layer_norm_rows(y, g_ref[...], b_ref[...])


def _out_ln(a, w, x2d, g, b):
    t, kin = a.shape
    d = x2d.shape[1]
    tm = ROW_TILE
    return pl.pallas_call(
        _out_ln_body,
        grid=(t // tm,),
        in_specs=[
            pl.BlockSpec((tm, kin), lambda i: (i, 0)),
            pl.BlockSpec((kin, d), lambda i: (0, 0)),
            pl.BlockSpec((tm, d), lambda i: (i, 0)),
            pl.BlockSpec((1, d), lambda i: (0, 0)),
            pl.BlockSpec((1, d), lambda i: (0, 0)),
        ],
        out_specs=pl.BlockSpec((tm, d), lambda i: (i, 0)),
        out_shape=jax.ShapeDtypeStruct((t, d), F32),
        compiler_params=_params("parallel"),
    )(a, w, x2d, g.reshape(1, d), b.reshape(1, d))


def _ffn_body(x_ref, halo_ref, wup_ref, cw_ref, cb_ref, wdn_ref, g_ref, b_ref, o_ref,
              xe_ref, h_ref, act_ref, *, tiles_per_seq):
    i = pl.program_id(0)
    tm = x_ref.shape[0]
    seq_start = (i % tiles_per_seq) == 0
    halo = halo_ref[...]
    xe_ref[0:HALO, :] = jnp.where(seq_start, 0.0, halo).astype(BF16)
    xe_ref[HALO:, :] = x_ref[...].astype(BF16)
    xe = xe_ref[...]

    def conv(c0, slot):
        h = h_ref.at[slot]
        h[...] = jnp.dot(xe, wup_ref[:, c0:c0 + FFN_CHUNK], preferred_element_type=F32)
        w = cw_ref[:, c0:c0 + FFN_CHUNK]
        return (h[HALO - 2:HALO - 2 + tm, :] * w[0:1, :]
                + h[HALO - 1:HALO - 1 + tm, :] * w[1:2, :]
                + h[HALO:HALO + tm, :] * w[2:3, :]
                + cb_ref[:, c0:c0 + FFN_CHUNK])

    for c in range(D_FF // FFN_CHUNK):
        c0 = c * FFN_CHUNK
        gate = conv(c0, 2 * (c % 2))
        up = conv(D_FF + c0, 2 * (c % 2) + 1)
        act_ref[:, c0:c0 + FFN_CHUNK] = (gate * jax.nn.sigmoid(gate) * up).astype(BF16)
    f = jnp.dot(act_ref[...], wdn_ref[...], preferred_element_type=F32)
    y = ALPHA * x_ref[...] + f
    o_ref[...] = _layer_norm_rows(y, g_ref[...], b_ref[...])


def _ffn_ln(x2d, seq, w_up, conv_w, conv_b, w_down, g, b):
    t, d = x2d.shape
    tm = FFN_ROW_TILE
    tiles_per_seq = seq // tm
    halo_blocks = tm // HALO
    resident = pl.Buffered(1)
    return pl.pallas_call(
        functools.partial(_ffn_body, tiles_per_seq=tiles_per_seq),
        grid=(t // tm,),
        in_specs=[
            pl.BlockSpec((tm, d), lambda i: (i, 0)),
            pl.BlockSpec((HALO, d), lambda i: (jnp.maximum(i * halo_blocks - 1, 0), 0)),
            pl.BlockSpec((d, 2 * D_FF), lambda i: (0, 0), pipeline_mode=resident),
            pl.BlockSpec((3, 2 * D_FF), lambda i: (0, 0)),
            pl.BlockSpec((1, 2 * D_FF), lambda i: (0, 0)),
            pl.BlockSpec((D_FF, d), lambda i: (0, 0), pipeline_mode=resident),
            pl.BlockSpec((1, d), lambda i: (0, 0)),
            pl.BlockSpec((1, d), lambda i: (0, 0)),
        ],
        out_specs=pl.BlockSpec((tm, d), lambda i: (i, 0)),
        out_shape=jax.ShapeDtypeStruct((t, d), F32),
        scratch_shapes=[
            pltpu.VMEM((tm + HALO, d), BF16),
            pltpu.VMEM((4, tm + HALO, FFN_CHUNK), F32),
            pltpu.VMEM((tm, D_FF), BF16),
        ],
        compiler_params=_params("parallel"),
    )(x2d, x2d, w_up.astype(BF16), conv_w, conv_b.reshape(1, -1), w_down.astype(BF16),
      g.reshape(1, d), b.reshape(1, d))


def _sortable_key(score):
    bits = lax.bitcast_convert_type(score, I32)
    key = jnp.where(bits < 0, bits ^ jnp.int32(0x7FFFFFFF), bits)
    return jnp.where(score == 0.0, 0, key)


def _dsa_body(q_ref, qi_ref, wi_ref, k_ref, ki_ref, v_ref, o_ref,
              key_ref, lhs_ref, wb_ref, qm_ref, cmp_ref, thr_ref, s_ref, p_ref, m_ref, acc_ref,
              *, topk):
    qn, tk = DSA_Q, DSA_TK
    groups = tk // LANES
    i = pl.program_id(1)
    nkt = (i * qn + qn + tk - 1) // tk
    lane = lax.broadcasted_iota(I32, (qn, LANES), 1)
    low = lane < 64
    row_pos = i * qn + lax.broadcasted_iota(I32, (qn, LANES), 0)

    for p in range(IDX_HEADS // 2):
        pair = qi_ref[:, p * LANES:(p + 1) * LANES]
        lhs_ref[2 * p * qn:(2 * p + 1) * qn, :] = jnp.where(low, pair, jnp.zeros_like(pair))
        lhs_ref[(2 * p + 1) * qn:(2 * p + 2) * qn, :] = jnp.where(low, jnp.zeros_like(pair), pair)
    for h in range(IDX_HEADS):
        wb_ref[h] = jnp.broadcast_to(wi_ref[:, h:h + 1], (qn, LANES))

    def index_tile(kt, carry):
        ki_t = ki_ref[pl.ds(pl.multiple_of(kt * tk, tk), tk), :]
        lg_all = _nt_dot(lhs_ref[...], ki_t)
        acc = [jnp.zeros((qn, LANES), F32) for _ in range(groups)]
        for h in range(IDX_HEADS):
            lg = lg_all[h * qn:(h + 1) * qn]
            wb = wb_ref[h]
            for g in range(groups):
                acc[g] = acc[g] + wb * jnp.maximum(lg[:, g * LANES:(g + 1) * LANES], 0.0)
        for g in range(groups):
            col = kt * tk + g * LANES + lane
            key_ref[kt, :, g * LANES:(g + 1) * LANES] = jnp.where(
                col <= row_pos, _sortable_key(acc[g]), INT_MIN)
        return carry

    lax.fori_loop(0, nkt, index_tile, 0)

    def count(pred):
        def body(kt, acc):
            ref_val = cmp_ref[...]
            for g in range(groups):
                keys = key_ref[kt, :, g * LANES:(g + 1) * LANES]
                acc = acc + jnp.where(pred(keys, ref_val, kt * tk + g * LANES + lane), 1.0, 0.0)
            return acc
        acc = lax.fori_loop(0, nkt, body, jnp.zeros((qn, LANES), F32))
        return jnp.sum(acc, axis=-1, keepdims=True)

    def bit_step(it, t_u):
        cand_u = t_u | jnp.left_shift(jnp.int32(1), 31 - it)
        cmp_ref[...] = jnp.broadcast_to(cand_u ^ INT_MIN, (qn, LANES))
        cnt = count(lambda keys, cand, pos: keys >= cand)
        return jnp.where(cnt >= topk, cand_u, t_u)

    t_u = lax.fori_loop(0, 32, bit_step, jnp.zeros((qn, 1), I32))
    thr1 = jnp.maximum(t_u ^ INT_MIN, INT_MIN + 1)
    thr_ref[...] = jnp.broadcast_to(thr1, (qn, LANES))
    cmp_ref[...] = thr_ref[...]
    cnt_gt = count(lambda keys, thr, pos: keys > thr)
    cnt_eq = count(lambda keys, thr, pos: keys == thr)
    need = topk - cnt_gt
    excess_ties = jnp.max(jnp.where(cnt_eq > need, 1.0, 0.0)) > 0.0

    @pl.when(excess_ties)
    def _():
        pos_bits = max(1, int(np.ceil(np.log2(key_ref.shape[0] * tk))))

        def pos_step(it, p_lim):
            cand1 = p_lim | jnp.left_shift(jnp.int32(1), pos_bits - 1 - it)
            cmp_ref[...] = jnp.broadcast_to(cand1, (qn, LANES))
            cnt = count(lambda keys, cand, pos: (keys == thr_ref[...]) & (pos < cand))
            return jnp.where(cnt < need, cand1, p_lim)

        p_lim = jnp.broadcast_to(
            lax.fori_loop(0, pos_bits, pos_step, jnp.zeros((qn, 1), I32)), (qn, LANES))

        def demote(kt, carry):
            thr = thr_ref[...]
            for g in range(groups):
                keys = key_ref[kt, :, g * LANES:(g + 1) * LANES]
                pos = kt * tk + g * LANES + lane
                key_ref[kt, :, g * LANES:(g + 1) * LANES] = jnp.where(
                    (keys == thr) & (pos > p_lim), thr - 1, keys)
            return carry

        lax.fori_loop(0, nkt, demote, 0)

    half = DSA_HEADS // 2
    eye = jnp.where(lax.broadcasted_iota(I32, (qn, LANES), 0) == lane, 1.0, 0.0).astype(BF16)
    for p in range(half):
        pair = q_ref[:, p * LANES:(p + 1) * LANES]
        zero = jnp.zeros_like(pair)
        qm_ref[p * qn:(p + 1) * qn, 0:LANES] = jnp.where(low, pair, zero)
        qm_ref[(half + p) * qn:(half + p + 1) * qn, 0:LANES] = jnp.where(low, zero, pair)
    for h in range(DSA_HEADS):
        qm_ref[h * qn:(h + 1) * qn, LANES:2 * LANES] = eye
    m_ref[...] = jnp.full(m_ref.shape, NEG, F32)
    acc_ref[...] = jnp.zeros(acc_ref.shape, F32)
    low_k = lax.broadcasted_iota(I32, (tk, LANES), 1) < 64

    def scores(kt_raw):
        kt = jnp.minimum(kt_raw, nkt - 1)
        thr_t = jnp.where(kt_raw < nkt, thr_ref[...], jnp.int32(2 ** 31 - 1))
        bias = jnp.concatenate(
            [jnp.where(key_ref[kt, :, g * LANES:(g + 1) * LANES] >= thr_t, 0.0, NEG)
             for g in range(groups)], axis=1)
        k_aug = jnp.concatenate(
            [k_ref[pl.ds(pl.multiple_of(kt * tk, tk), tk), :], bias.T.astype(BF16)], axis=1)
        return _nt_dot(qm_ref[...], k_aug)

    def attend_tile(kt, carry):
        s_ref[0] = scores(kt)
        v_t = v_ref[pl.ds(pl.multiple_of(kt * tk, tk), tk), :]
        one = jnp.ones_like(v_t)
        for h in range(DSA_HEADS):
            rows = slice(h * qn, (h + 1) * qn)
            sg = [s_ref[0, rows, g * LANES:(g + 1) * LANES] for g in range(groups)]
            m_old = m_ref[0, h]
            m_new = jnp.maximum(m_old, jnp.max(functools.reduce(jnp.maximum, sg), axis=-1, keepdims=True))
            m_ref[1, h] = jnp.exp2(m_old - m_new)
            m_ref[0, h] = m_new
            for g in range(groups):
                p_ref[rows, g * LANES:(g + 1) * LANES] = jnp.exp2(sg[g] - m_new).astype(BF16)
        pv = (jnp.dot(p_ref[0:half * qn, :], jnp.where(low_k, v_t, one), preferred_element_type=F32),
              jnp.dot(p_ref[half * qn:, :], jnp.where(low_k, one, v_t), preferred_element_type=F32))
        for h in range(DSA_HEADS):
            acc_ref[h] = m_ref[1, h] * acc_ref[h] + pv[h // half][(h % half) * qn:(h % half + 1) * qn]
        return carry

    lax.fori_loop(0, nkt, attend_tile, 0)

    for p in range(half):
        a0 = acc_ref[p]
        a1 = acc_ref[half + p]
        o0 = a0 / pltpu.roll(a0, 64, 1)
        o1 = a1 / pltpu.roll(a1, 64, 1)
        o_ref[:, p * LANES:(p + 1) * LANES] = jnp.where(low, o0, o1).astype(o_ref.dtype)


def _dsa_attention(proj, wi, bsz, seq, topk):
    t = bsz * seq
    qn, tk = DSA_Q, DSA_TK
    nqb = seq // qn
    nkt_max = seq // tk
    qcols = DSA_HEADS * DSA_HEAD_DIM
    icols = IDX_HEADS * IDX_DIM
    kcol = (qcols + icols) // LANES
    return pl.pallas_call(
        functools.partial(_dsa_body, topk=topk),
        grid=(bsz, nqb),
        in_specs=[
            pl.BlockSpec((qn, qcols), lambda b, i: (b * nqb + i, 0)),
            pl.BlockSpec((qn, icols), lambda b, i: (b * nqb + i, qcols // icols)),
            pl.BlockSpec((qn, LANES), lambda b, i: (b * nqb + i, 0)),
            pl.BlockSpec((seq, LANES), lambda b, i: (b, kcol)),
            pl.BlockSpec((seq, LANES), lambda b, i: (b, kcol + 1)),
            pl.BlockSpec((seq, LANES), lambda b, i: (b, kcol + 2)),
        ],
        out_specs=pl.BlockSpec((qn, qcols), lambda b, i: (b * nqb + i, 0)),
        out_shape=jax.ShapeDtypeStruct((t, qcols), BF16),
        scratch_shapes=[
            pltpu.VMEM((nkt_max, qn, tk), I32),
            pltpu.VMEM((IDX_HEADS * qn, LANES), BF16),
            pltpu.VMEM((IDX_HEADS, qn, LANES), F32),
            pltpu.VMEM((DSA_HEADS * qn, 2 * LANES), BF16),
            pltpu.VMEM((qn, LANES), I32),
            pltpu.VMEM((qn, LANES), I32),
            pltpu.VMEM((2, DSA_HEADS * qn, tk), F32),
            pltpu.VMEM((DSA_HEADS * qn, tk), BF16),
            pltpu.VMEM((2, DSA_HEADS, qn, LANES), F32),
            pltpu.VMEM((DSA_HEADS, qn, LANES), F32),
        ],
        compiler_params=_params("parallel", "arbitrary"),
    )(proj, proj, wi, proj, proj, proj)


def _dsa_layer(x2d, bsz, seq, w_in, w_out, ln_g, ln_b):
    h, dh, hi, di = DSA_HEADS, DSA_HEAD_DIM, IDX_HEADS, IDX_DIM
    offs = np.cumsum([h * dh, dh, dh, hi * di, di, hi])
    wq, wk, wv, wqi, wki, wwi = (w_in[:, a:b] for a, b in zip([0] + offs[:-1].tolist(), offs.tolist()))
    wwi_pad = jnp.concatenate([wwi, jnp.zeros((D_MODEL, LANES - hi), F32)], axis=1)
    w_main = jnp.concatenate([wq, wqi, wk, wk, wki, wki, wv, wv, wwi_pad], axis=1).astype(BF16)
    cos, sin = _rope_tables(seq, dh)
    scale = dh ** -0.5 * LOG2E
    chunks = [(0, 512, "rope64", scale), (512, 512, "rope64", scale), (1024, 512, "rope64", 1.0),
              (1536, 256, "rope64", 1.0), (1792, 256, "plain", 1.0)]
    proj, wi = _proj(x2d, w_main, cos, sin, seq, chunks, BF16,
                     side=(w_main.shape[1] - LANES, LANES, hi ** -0.5 * di ** -0.5))
    o = _dsa_attention(proj, wi, bsz, seq, min(DSA_TOPK, seq // 4))
    return _out_ln(o, w_out.astype(BF16), x2d, ln_g, ln_b)


def _moba_body(q_ref, k_ref, v_ref, km_ref, o_ref, qa_ref, m_ref, acc_ref, *, n_sel):
    blk = MOBA_BLOCK
    nq = MOBA_QBLOCKS_PER_STEP
    rows_n = nq * blk
    first_blk = pl.program_id(2) * nq
    lane = lax.broadcasted_iota(I32, (rows_n, LANES), 1)
    low = lane < 64
    own_blk = first_blk + lax.broadcasted_iota(I32, (rows_n, LANES), 0) // blk

    def own(h, mine, other):
        m = low if mine.shape[0] == rows_n else low_k
        return jnp.where(m, mine, other) if h == 0 else jnp.where(m, other, mine)

    zero = jnp.zeros((rows_n, LANES), BF16)
    one_k = jnp.ones((blk, LANES), BF16)
    r_loc = lax.broadcasted_iota(I32, (blk, blk), 0)
    c_loc = lax.broadcasted_iota(I32, (blk, blk), 1)
    lane_f = lane.astype(F32)
    lane_k = lax.broadcasted_iota(I32, (blk, LANES), 1)
    low_k = lane_k < 64
    heads = 2 * MOBA_PAIRS_PER_STEP

    def pair_lanes(hd):
        return slice((hd // 2) * LANES, (hd // 2 + 1) * LANES)

    def key_block(hd, j, with_block_id):
        h = hd % 2
        k_t = k_ref[pl.ds(pl.multiple_of(j * blk, blk), blk), pair_lanes(hd)]
        if not with_block_id:
            return k_t
        block_lane = jnp.where(lane_k == (j + 64 if h == 0 else j), 1.0, 0.0).astype(BF16)
        return own(h, k_t, block_lane)

    def value_block(hd, j):
        return own(hd % 2, v_ref[pl.ds(pl.multiple_of(j * blk, blk), blk), pair_lanes(hd)], one_k)

    for hd in range(heads):
        h = hd % 2
        pair = q_ref[:, pair_lanes(hd)]
        km = km_ref[0, :, pair_lanes(hd)].astype(BF16)
        qh = own(h, pair, zero)
        gate = jnp.where(lane < own_blk, _nt_dot(qh, km), NEG)
        chosen = lane < 0
        for _ in range(n_sel):
            best = jnp.max(gate, axis=-1, keepdims=True)
            first = jnp.min(jnp.where(gate == best, lane_f, float(LANES)), axis=-1, keepdims=True)
            hit = lane_f == first
            chosen = jnp.logical_or(chosen, hit)
            gate = jnp.where(hit, -jnp.inf, gate)
        pen = jnp.where(jnp.logical_and(chosen, lane < own_blk), 0.0, NEG)
        if h == 0:
            pen = pltpu.roll(pen, 64, 1)
        qa = own(h, pair, pen.astype(BF16))
        qa_ref[hd] = qa

        for a in range(nq):
            rs = slice(a * blk, (a + 1) * blk)
            s = [_nt_dot(qa[rs], key_block(hd, first_blk + bb, True)) for bb in range(a)]
            s.append(jnp.where(c_loc <= r_loc, _nt_dot(qh[rs], key_block(hd, first_blk + a, False)), NEG))
            m0 = jnp.max(functools.reduce(jnp.maximum, s), axis=-1, keepdims=True)
            m_ref[hd, rs, :] = jnp.broadcast_to(m0, (blk, LANES))
            acc_ref[hd, rs, :] = functools.reduce(jnp.add, [
                jnp.dot(jnp.exp2(s[bb] - m0).astype(BF16), value_block(hd, first_blk + bb),
                        preferred_element_type=F32) for bb in range(a + 1)])

    def past_pair(t, carry):
        for hd in range(heads):
            s = [_nt_dot(qa_ref[hd], key_block(hd, 2 * t + u, True)) for u in range(2)]
            m_old = m_ref[hd]
            m_new = jnp.maximum(m_old, jnp.max(jnp.maximum(s[0], s[1]), axis=-1, keepdims=True))
            alpha = jnp.exp2(m_old - m_new)
            m_ref[hd] = m_new
            pv = [jnp.dot(jnp.exp2(s[u] - m_new[:, 0:1]).astype(BF16), value_block(hd, 2 * t + u),
                          preferred_element_type=F32) for u in range(2)]
            acc_ref[hd] = alpha * acc_ref[hd] + (pv[0] + pv[1])
        return carry

    lax.fori_loop(0, first_blk // 2, past_pair, 0)
    for pp in range(MOBA_PAIRS_PER_STEP):
        a0 = acc_ref[2 * pp]
        a1 = acc_ref[2 * pp + 1]
        o_ref[:, pp * LANES:(pp + 1) * LANES] = jnp.where(
            low, a0 / pltpu.roll(a0, 64, 1), a1 / pltpu.roll(a1, 64, 1)).astype(o_ref.dtype)


def _moba_attention(proj, kmean, bsz, seq):
    t = bsz * seq
    rows_n = MOBA_QBLOCKS_PER_STEP * MOBA_BLOCK
    nsb = seq // rows_n
    width = MOBA_PAIRS_PER_STEP * LANES
    groups = MOBA_HEADS * MOBA_HEAD_DIM // width
    heads = 2 * MOBA_PAIRS_PER_STEP
    n_sel = min(MOBA_TOPK, seq // MOBA_BLOCK - 1)
    return pl.pallas_call(
        functools.partial(_moba_body, n_sel=n_sel),
        grid=(bsz, groups, nsb),
        in_specs=[
            pl.BlockSpec((rows_n, width), lambda b, p, i: (b * nsb + i, p)),
            pl.BlockSpec((seq, width), lambda b, p, i: (b, groups + p)),
            pl.BlockSpec((seq, width), lambda b, p, i: (b, 2 * groups + p)),
            pl.BlockSpec((1, LANES, width), lambda b, p, i: (b, 0, p)),
        ],
        out_specs=pl.BlockSpec((rows_n, width), lambda b, p, i: (b * nsb + i, p)),
        out_shape=jax.ShapeDtypeStruct((t, MOBA_HEADS * MOBA_HEAD_DIM), BF16),
        scratch_shapes=[
            pltpu.VMEM((heads, rows_n, LANES), BF16),
            pltpu.VMEM((heads, rows_n, LANES), F32),
            pltpu.VMEM((heads, rows_n, LANES), F32),
        ],
        compiler_params=_params("parallel", "parallel", "arbitrary"),
    )(proj, proj, proj, kmean)


def _moba_layer(x2d, bsz, seq, w_in, w_out, ln_g, ln_b):
    hd = MOBA_HEADS * MOBA_HEAD_DIM
    cos, sin = _rope_tables(seq, MOBA_HEAD_DIM)
    scale = MOBA_HEAD_DIM ** -0.5 * LOG2E
    chunks = [(c0, 512, "rope64", scale if c0 < hd else 1.0) for c0 in range(0, 2 * hd, 512)]
    chunks += [(c0, 512, "plain", 1.0) for c0 in range(2 * hd, 3 * hd, 512)]
    proj, kmean = _proj(x2d, w_in.astype(BF16), cos, sin, seq, chunks, BF16,
                        block_mean=(hd, 2 * hd, MOBA_BLOCK))
    nb = seq // MOBA_BLOCK
    kmean = jnp.pad(kmean.reshape(bsz, nb, hd), ((0, 0), (0, LANES - nb), (0, 0)))
    o = _moba_attention(proj, kmean, bsz, seq)
    return _out_ln(o, w_out.astype(BF16), x2d, ln_g, ln_b)


def _ret_body(q_ref, k_ref, v_ref, g_ref, lg_ref, gng_ref, gnb_ref, o_ref, state_ref):
    c = pl.program_id(2)
    cc = RET_CHUNK

    @pl.when(c == 0)
    def _():
        state_ref[...] = jnp.zeros(state_ref.shape, F32)

    lg = lg_ref[0]
    ri = lax.broadcasted_iota(I32, (cc, cc), 0)
    ci = lax.broadcasted_iota(I32, (cc, cc), 1)
    rel = (ri - ci).astype(F32)
    lg_row = jnp.concatenate([lg] * (cc // LANES), axis=1)
    decay = jnp.where(rel >= 0, jnp.exp(jnp.maximum(rel, 0.0) * lg_row), 0.0)
    pos = lax.broadcasted_iota(I32, (cc, LANES), 0).astype(F32)
    q_decay = jnp.exp((pos + 1.0) * lg)[:, 0:1]
    k_decay = jnp.exp((cc - 1.0 - pos) * lg)[:, 0:1]
    chunk_decay = jnp.exp(cc * lg)[:, 0:1]

    q = q_ref[...]
    k = k_ref[...]
    v = v_ref[...]
    state = state_ref[...]
    inner = jnp.dot((_nt_dot(q, k) * decay).astype(BF16), v, preferred_element_type=F32)
    cross = jnp.dot(q, state.astype(BF16), preferred_element_type=F32) * q_decay
    kd_t = (k.astype(F32) * k_decay).T.astype(BF16)
    state_ref[...] = state * chunk_decay + jnp.dot(kd_t, v, preferred_element_type=F32)

    o = inner + cross
    mu = jnp.mean(o, axis=-1, keepdims=True)
    d = o - mu
    var = jnp.mean(d * d, axis=-1, keepdims=True)
    on = d * lax.rsqrt(var + GN_EPS) * gng_ref[...] + gnb_ref[...]
    gate = g_ref[...]
    o_ref[...] = (gate * jax.nn.sigmoid(gate) * on).astype(o_ref.dtype)


def _retention(proj, gate, lg, gn_g, gn_b, bsz, seq):
    t = bsz * seq
    cc = RET_CHUNK
    nc = seq // cc
    h, dk, dv = RET_HEADS, RET_QK_DIM, RET_V_DIM
    return pl.pallas_call(
        _ret_body,
        grid=(bsz, h, nc),
        in_specs=[
            pl.BlockSpec((cc, dk), lambda b, hh, c: (b * nc + c, hh)),
            pl.BlockSpec((cc, dk), lambda b, hh, c: (b * nc + c, h + hh)),
            pl.BlockSpec((cc, dv), lambda b, hh, c: (b * nc + c, (2 * h * dk) // dv + hh)),
            pl.BlockSpec((cc, dv), lambda b, hh, c: (b * nc + c, hh)),
            pl.BlockSpec((1, 1, LANES), lambda b, hh, c: (hh, 0, 0)),
            pl.BlockSpec((1, dv), lambda b, hh, c: (0, hh)),
            pl.BlockSpec((1, dv), lambda b, hh, c: (0, hh)),
        ],
        out_specs=pl.BlockSpec((cc, dv), lambda b, hh, c: (b * nc + c, hh)),
        out_shape=jax.ShapeDtypeStruct((t, h * dv), BF16),
        scratch_shapes=[pltpu.VMEM((dk, dv), F32)],
        compiler_params=_params("parallel", "parallel", "arbitrary"),
    )(proj, proj, proj, gate, lg, gn_g.reshape(1, -1), gn_b.reshape(1, -1))


def _retention_layer(x2d, bsz, seq, w_in, gn_g, gn_b, w_out, ln_g, ln_b):
    h, dk, dv = RET_HEADS, RET_QK_DIM, RET_V_DIM
    cos, sin = _rope_tables(seq, dk)
    nqk = 2 * h * dk
    chunks = [(c0, 512, "rope256", 1.0 if c0 < h * dk else dk ** -0.5) for c0 in range(0, nqk, 512)]
    chunks += [(c0, 512, "plain", 1.0) for c0 in range(nqk, nqk + h * dv, 512)]
    proj = _proj(x2d, w_in[:, :nqk + h * dv].astype(BF16), cos, sin, seq, chunks, BF16)
    gchunks = [(c0, 512, "plain", 1.0) for c0 in range(0, h * dv, 512)]
    gate = _proj(x2d, w_in[:, nqk + h * dv:].astype(BF16), cos, sin, seq, gchunks, F32)
    log_gamma = jnp.log1p(-(2.0 ** (-5.0 - jnp.arange(h, dtype=F32))))
    lg = jnp.broadcast_to(log_gamma[:, None, None], (h, 1, LANES))
    y = _retention(proj, gate, lg, gn_g, gn_b, bsz, seq)
    return _out_ln(y, w_out.astype(BF16), x2d, ln_g, ln_b)


def kernel(x, l0_w_in, l0_w_out, l0_ln1_g, l0_ln1_b, l0_w_up, l0_conv_w, l0_conv_b, l0_w_down, l0_ln2_g, l0_ln2_b, l1_w_in, l1_w_out, l1_ln1_g, l1_ln1_b, l1_w_up, l1_conv_w, l1_conv_b, l1_w_down, l1_ln2_g, l1_ln2_b, l2_w_in, l2_gn_g, l2_gn_b, l2_w_out, l2_ln1_g, l2_ln1_b, l2_w_up, l2_conv_w, l2_conv_b, l2_w_down, l2_ln2_g, l2_ln2_b, l3_w_in, l3_w_out, l3_ln1_g, l3_ln1_b, l3_w_up, l3_conv_w, l3_conv_b, l3_w_down, l3_ln2_g, l3_ln2_b):
    bsz, seq, d = x.shape
    h = x.reshape(bsz * seq, d)
    h = _dsa_layer(h, bsz, seq, l0_w_in, l0_w_out, l0_ln1_g, l0_ln1_b)
    h = _ffn_ln(h, seq, l0_w_up, l0_conv_w, l0_conv_b, l0_w_down, l0_ln2_g, l0_ln2_b)
    h = _moba_layer(h, bsz, seq, l1_w_in, l1_w_out, l1_ln1_g, l1_ln1_b)
    h = _ffn_ln(h, seq, l1_w_up, l1_conv_w, l1_conv_b, l1_w_down, l1_ln2_g, l1_ln2_b)
    h = _retention_layer(h, bsz, seq, l2_w_in, l2_gn_g, l2_gn_b, l2_w_out, l2_ln1_g, l2_ln1_b)
    h = _ffn_ln(h, seq, l2_w_up, l2_conv_w, l2_conv_b, l2_w_down, l2_ln2_g, l2_ln2_b)
    h = _dsa_layer(h, bsz, seq, l3_w_in, l3_w_out, l3_ln1_g, l3_ln1_b)
    h = _ffn_ln(h, seq, l3_w_up, l3_conv_w, l3_conv_b, l3_w_down, l3_ln2_g, l3_ln2_b)
    return h.reshape(bsz, seq, d)
```
